```python
import math
import jax
import jax.numpy as jnp
from jax import lax
import numpy as np

D_MODEL = 1024
BATCH = 16
SEQ = 4096
DEPTH = 4

HEAD_DIM = 64
MOBA_HEADS = 8
MOBA_BLOCK = 256
MOBA_TOPK = 3
MOBA_Q_CHUNK = 32
SSM_D_INNER = D_MODEL
SSM_HEADDIM = 64
SSM_HEADS = SSM_D_INNER // SSM_HEADDIM
SSM_GROUPS = 4
SSM_STATE = 128
SSM_CONV = 4
SSM_CHUNK = 128
DT_MIN = 0.001
DT_MAX = 0.1
A_INIT_MAX = 16.0
DIL_HEADS = D_MODEL // HEAD_DIM
DIL_BRANCHES = ((128, 1), (512, 4), (2048, 16))
DIL_BLOCK = 128
REL_BUCKETS = 32
REL_MAX_DIST = 2048
REL_HEADS = max(MOBA_HEADS, DIL_HEADS)
FFN_DENSE = 2816
N_EXPERTS = 8
MOE_TOPK = 2
FFN_EXPERT = 3584
DEEPNORM_ALPHA = (2.0 * DEPTH) ** 0.25
DEEPNORM_BETA = (8.0 * DEPTH) ** -0.25
LN_EPS = 1e-5
RMS_EPS = 1e-5
N_EVEN = (DEPTH + 1) // 2
N_ODD = DEPTH // 2
MOBA_W = MOBA_HEADS * HEAD_DIM
SSM_BC = SSM_GROUPS * SSM_STATE
SSM_XBC = SSM_D_INNER + 2 * SSM_BC
EVEN_SPLITS = (MOBA_W, 2 * MOBA_W, 3 * MOBA_W, 3 * MOBA_W + SSM_D_INNER, 3 * MOBA_W + SSM_D_INNER + SSM_XBC)
EVEN_IN = 3 * MOBA_W + SSM_D_INNER + SSM_XBC + SSM_HEADS
EVEN_MIX = MOBA_W + SSM_D_INNER
DIL_W = DIL_HEADS * HEAD_DIM

kernel_name = "hybrid_moba_ssd_dilated_moe_trunk"


def layer_norm(x, g, b):
    xf = x.astype(jnp.float32)
    mu = jnp.mean(xf, axis=-1, keepdims=True)
    var = jnp.mean(jnp.square(xf - mu), axis=-1, keepdims=True)
    return ((xf - mu) * lax.rsqrt(var + LN_EPS) * g.astype(jnp.float32) + b.astype(jnp.float32)).astype(x.dtype)


def gated_rms_norm(y, z, g):
    B, S, Din = y.shape
    yf = (y.astype(jnp.float32) * jax.nn.silu(z.astype(jnp.float32))).reshape(B, S, SSM_GROUPS, Din // SSM_GROUPS)
    yf = yf * lax.rsqrt(jnp.mean(yf * yf, axis=-1, keepdims=True) + RMS_EPS)
    return (yf.reshape(B, S, Din) * g.astype(jnp.float32)).astype(y.dtype)


def rel_bucket(dist):
    max_exact = REL_BUCKETS // 2
    d = jnp.maximum(dist, 0)
    df = jnp.maximum(d, 1).astype(jnp.float32)
    large = max_exact + (jnp.log(df / max_exact) / math.log(REL_MAX_DIST / max_exact)
                         * (REL_BUCKETS - max_exact)).astype(jnp.int32)
    large = jnp.minimum(large, REL_BUCKETS - 1)
    return jnp.where(d < max_exact, d, large)


def moba_attention(q, k, v, rel_bias):
    B, S, H, hd = q.shape
    nblk = -(-S // MOBA_BLOCK)
    pad = nblk * MOBA_BLOCK - S
    kk = min(MOBA_TOPK, nblk)
    scale = hd ** -0.5
    qh = q.transpose(0, 2, 1, 3)

    def blocks(a):
        a = jnp.pad(a, ((0, 0), (0, pad), (0, 0), (0, 0))).transpose(0, 2, 1, 3)
        return a.reshape(B, H, nblk, MOBA_BLOCK, hd)

    kb, vb = blocks(k), blocks(v)
    k_mean = jnp.mean(kb.astype(jnp.float32), axis=3)
    pos = jnp.arange(S, dtype=jnp.int32)
    gate = jnp.einsum('bhsd,bhnd->bhsn', qh.astype(jnp.float32), k_mean)
    fully_past = jnp.arange(nblk, dtype=jnp.int32)[None, :] < (pos // MOBA_BLOCK)[:, None]
    gate = jnp.where(fully_past, gate, -jnp.inf)
    sel_val, sel_idx = lax.top_k(gate, kk)
    sel_ok = sel_val > -jnp.inf
    bias_h = rel_bias[:, :H].T.astype(jnp.float32)
    n_chunks = S // MOBA_Q_CHUNK
    b_ix = jnp.arange(B)[:, None, None, None]
    h_ix = jnp.arange(H)[None, :, None, None]
    key_off = jnp.arange(MOBA_BLOCK, dtype=jnp.int32)

    def to_chunks(a):
        a = a.reshape(B, H, n_chunks, MOBA_Q_CHUNK, *a.shape[3:])
        return jnp.moveaxis(a, 2, 0)

    def attend_chunk(args):
        c, qc, idx, ok = args
        qpos = c * MOBA_Q_CHUNK + jnp.arange(MOBA_Q_CHUNK, dtype=jnp.int32)
        own = (c * MOBA_Q_CHUNK) // MOBA_BLOCK
        k_own = lax.dynamic_index_in_dim(kb, own, axis=2, keepdims=False)
        v_own = lax.dynamic_index_in_dim(vb, own, axis=2, keepdims=False)
        dist_own = qpos[:, None] - (own * MOBA_BLOCK + key_off)[None, :]
        s_own = jnp.einsum('bhqd,bhjd->bhqj', qc, k_own).astype(jnp.float32) * scale
        s_own = s_own + bias_h[:, rel_bucket(dist_own)]
        s_own = jnp.where(dist_own >= 0, s_own, -jnp.inf)
        k_sel = kb[b_ix, h_ix, idx]
        v_sel = vb[b_ix, h_ix, idx]
        dist_sel = qpos[:, None, None] - (idx[..., None] * MOBA_BLOCK + key_off)
        s_sel = jnp.einsum('bhqd,bhqkjd->bhqkj', qc, k_sel).astype(jnp.float32) * scale
        s_sel = s_sel + bias_h[h_ix[..., None], rel_bucket(dist_sel)]
        s_sel = jnp.where(ok[..., None], s_sel, -jnp.inf)
        n_q = qc.shape[2]
        scores = jnp.concatenate([s_own, s_sel.reshape(B, H, n_q, kk * MOBA_BLOCK)], axis=-1)
        p = jax.nn.softmax(scores, axis=-1).astype(v.dtype)
        p_own = p[..., :MOBA_BLOCK]
        p_sel = p[..., MOBA_BLOCK:].reshape(B, H, n_q, kk, MOBA_BLOCK)
        return (jnp.einsum('bhqj,bhjd->bhqd', p_own, v_own)
                + jnp.einsum('bhqkj,bhqkjd->bhqd', p_sel, v_sel))

    out = lax.map(attend_chunk, (jnp.arange(n_chunks, dtype=jnp.int32), to_chunks(qh),
                                 to_chunks(sel_idx), to_chunks(sel_ok)))
    out = jnp.moveaxis(out, 0, 2).reshape(B, H, S, hd)
    return out.transpose(0, 2, 1, 3)


def causal_depthwise_conv(u, w, b):
    out = lax.conv_general_dilated(u, w[:, None, :].astype(u.dtype), window_strides=(1,),
                                   padding=((SSM_CONV - 1, 0),),
                                   dimension_numbers=('NWC', 'WIO', 'NWC'),
                                   feature_group_count=u.shape[-1])
    return out + b.astype(u.dtype)


def ssd_scan(xh, dt, A, Bg, Cg):
    Bsz, S, H, P = xh.shape
    G, N = Bg.shape[2], Bg.shape[3]
    R = H // G
    L = SSM_CHUNK
    nc = S // L
    x = (xh * dt[..., None]).reshape(Bsz, nc, L, G, R, P)
    a = (dt * A).reshape(Bsz, nc, L, G, R)
    Bc = Bg.reshape(Bsz, nc, L, G, N)
    Cc = Cg.reshape(Bsz, nc, L, G, N)
    a_cs = jnp.cumsum(a, axis=2)
    causal = jnp.tril(jnp.ones((L, L), dtype=bool))
    seg = a_cs[:, :, :, None] - a_cs[:, :, None]
    decay = jnp.exp(jnp.where(causal[:, :, None, None], seg, -jnp.inf))
    cb = jnp.einsum('bclgn,bcsgn->bclsg', Cc, Bc)
    y_diag = jnp.einsum('bclsg,bclsgr,bcsgrp->bclgrp', cb, decay, x)
    decay_to_end = jnp.exp(a_cs[:, :, -1:] - a_cs)
    chunk_states = jnp.einsum('bclgn,bclgr,bclgrp->bcgrpn', Bc, decay_to_end, x)
    chunk_decay = jnp.exp(a_cs[:, :, -1])

    def step(h, inp):
        st, dec = inp
        return h * dec[..., None, None] + st, h

    h0 = jnp.zeros((Bsz, G, R, P, N), jnp.float32)
    _, h_in = lax.scan(step, h0, (jnp.moveaxis(chunk_states, 1, 0), jnp.moveaxis(chunk_decay, 1, 0)))
    h_in = jnp.moveaxis(h_in, 0, 1)
    y_off = jnp.einsum('bclgn,bcgrpn,bclgr->bclgrp', Cc, h_in, jnp.exp(a_cs))
    return (y_diag + y_off).reshape(Bsz, S, H, P)


def mamba2_mixer(z, xbc, dt_raw, conv_w, conv_b, dt_bias, A_log, D_skip, norm_g):
    Bsz, S, _ = z.shape
    xbc = jax.nn.silu(causal_depthwise_conv(xbc, conv_w, conv_b))
    xs, Bm, Cm = jnp.split(xbc, (SSM_D_INNER, SSM_D_INNER + SSM_BC), axis=-1)
    xs = xs.reshape(Bsz, S, SSM_HEADS, SSM_HEADDIM).astype(jnp.float32)
    dt = jax.nn.softplus(dt_raw.astype(jnp.float32) + dt_bias.astype(jnp.float32))
    A = -jnp.exp(A_log.astype(jnp.float32))
    y = ssd_scan(xs, dt, A,
                 Bm.reshape(Bsz, S, SSM_GROUPS, SSM_STATE).astype(jnp.float32),
                 Cm.reshape(Bsz, S, SSM_GROUPS, SSM_STATE).astype(jnp.float32))
    y = y + D_skip.astype(jnp.float32)[:, None] * xs
    return gated_rms_norm(y.reshape(Bsz, S, SSM_D_INNER).astype(z.dtype), z, norm_g)


def dilated_attention(q, k, v, rel_bias):
    B, S, H, hd = q.shape
    scale = hd ** -0.5
    bias_h = rel_bias[:, :H].T.astype(jnp.float32)
    qi = jnp.arange(DIL_BLOCK, dtype=jnp.int32)[:, None]
    ki = jnp.arange(2 * DIL_BLOCK, dtype=jnp.int32)[None, :]
    steps = qi + DIL_BLOCK - ki
    outs, lses = [], []
    for window, dil in DIL_BRANCHES:
        span = window // dil
        L = S // dil
        nb = -(-L // DIL_BLOCK)
        Lp = nb * DIL_BLOCK

        def by_residue(a):
            a = a.reshape(B, L, dil, H, hd).transpose(0, 3, 2, 1, 4)
            return jnp.pad(a, ((0, 0), (0, 0), (0, 0), (0, Lp - L), (0, 0)))

        def banded(a):
            a = jnp.pad(by_residue(a), ((0, 0), (0, 0), (0, 0), (DIL_BLOCK, 0), (0, 0)))
            a = a.reshape(B, H, dil, nb + 1, DIL_BLOCK, hd)
            return jnp.concatenate([a[:, :, :, :-1], a[:, :, :, 1:]], axis=-2)

        qb = by_residue(q).reshape(B, H, dil, nb, DIL_BLOCK, hd)
        kb, vb = banded(k), banded(v)
        key_idx = jnp.arange(nb, dtype=jnp.int32)[:, None, None] * DIL_BLOCK + ki[None] - DIL_BLOCK
        valid = (steps >= 0) & (steps <= span) & (key_idx >= 0)
        s = jnp.einsum('bhrnqd,bhrnkd->bhrnqk', qb, kb).astype(jnp.float32) * scale
        s = s + bias_h[:, rel_bucket(steps * dil)][None, :, None, None]
        s = jnp.where(valid, s, -jnp.inf)
        m = jnp.max(s, axis=-1, keepdims=True)
        e = jnp.exp(s - m)
        den = jnp.sum(e, axis=-1)
        o = jnp.einsum('bhrnqk,bhrnkd->bhrnqd', e.astype(v.dtype), vb).astype(jnp.float32) / den[..., None]
        lse = m[..., 0] + jnp.log(den)
        o = o.reshape(B, H, dil, Lp, hd)[:, :, :, :L].transpose(0, 3, 2, 1, 4).reshape(B, S, H, hd)
        lse = lse.reshape(B, H, dil, Lp)[:, :, :, :L].transpose(0, 3, 2, 1).reshape(B, S, H)
        outs.append(o)
        lses.append(lse)
    w = jax.nn.softmax(jnp.stack(lses), axis=0)
    return jnp.sum(w[..., None] * jnp.stack(outs), axis=0).astype(q.dtype)


def even_mixer(x, w_in, conv_w, conv_b, dt_bias, A_log, D_skip, norm_g, w_out, rel_bias):
    B, S, _ = x.shape
    proj = x @ w_in
    q, k, v, z, xbc, dt_raw = jnp.split(proj, EVEN_SPLITS, axis=-1)
    shp = (B, S, MOBA_HEADS, HEAD_DIM)
    attn = moba_attention(q.reshape(shp), k.reshape(shp), v.reshape(shp), rel_bias).reshape(B, S, MOBA_W)
    ssm = mamba2_mixer(z, xbc, dt_raw, conv_w, conv_b, dt_bias, A_log, D_skip, norm_g)
    return jnp.concatenate([attn, ssm], axis=-1) @ w_out


def odd_mixer(x, w_qkv, w_out, rel_bias):
    B, S, _ = x.shape
    q, k, v = jnp.split(x @ w_qkv, 3, axis=-1)
    shp = (B, S, DIL_HEADS, HEAD_DIM)
    o = dilated_attention(q.reshape(shp), k.reshape(shp), v.reshape(shp), rel_bias)
    return o.reshape(B, S, DIL_W) @ w_out


def swiglu(x, w_in, w_down):
    g, u = jnp.split(x @ w_in, 2, axis=-1)
    return (jax.nn.silu(g) * u) @ w_down


def moe_swiglu(x, w_router, w_in, w_down):
    logits = (x @ w_router).astype(jnp.float32)
    top_val, top_idx = lax.top_k(logits, MOE_TOPK)
    gates = jax.nn.softmax(top_val, axis=-1)
    combine = jnp.sum(jax.nn.one_hot(top_idx, N_EXPERTS, dtype=jnp.float32) * gates[..., None], axis=-2)

    def per_row(args):
        xr, cr = args
        g, u = jnp.split(jnp.einsum('sd,edf->sef', xr, w_in), 2, axis=-1)
        act = jax.nn.silu(g) * u * cr.astype(xr.dtype)[..., None]
        return jnp.einsum('sef,efd->sd', act, w_down)

    return lax.map(per_row, (x, combine))


def setup_inputs(seed: int = 0) -> dict:
    key = jax.random.key(seed)
    ks = iter(jax.random.split(key, 32))

    def nrm(shape, scale):
        return jax.random.normal(next(ks), shape, jnp.float32) * scale

    D = D_MODEL
    x = nrm((BATCH, SEQ, D), 1.0)
    rel_bias = nrm((REL_BUCKETS, REL_HEADS), 0.5)
    ln_mix_g = 1.0 + nrm((DEPTH, D), 0.02)
    ln_mix_b = nrm((DEPTH, D), 0.02)
    ln_ffn_g = 1.0 + nrm((DEPTH, D), 0.02)
    ln_ffn_b = nrm((DEPTH, D), 0.02)
    even_w_in = nrm((N_EVEN, D, EVEN_IN), D ** -0.5)
    even_conv_w = nrm((N_EVEN, SSM_CONV, SSM_XBC), SSM_CONV ** -0.5)
    even_conv_b = nrm((N_EVEN, SSM_XBC), 0.02)
    u = jax.random.uniform(next(ks), (N_EVEN, SSM_HEADS), jnp.float32)
    dt0 = jnp.exp(u * (math.log(DT_MAX) - math.log(DT_MIN)) + math.log(DT_MIN))
    even_dt_bias = dt0 + jnp.log(-jnp.expm1(-dt0))
    even_A_log = jnp.log(jax.random.uniform(next(ks), (N_EVEN, SSM_HEADS), jnp.float32, 1.0, A_INIT_MAX))
    even_D = 1.0 + nrm((N_EVEN, SSM_HEADS), 0.02)
    even_norm_g = 1.0 + nrm((N_EVEN, SSM_D_INNER), 0.02)
    even_w_out = nrm((N_EVEN, EVEN_MIX, D), EVEN_MIX ** -0.5 * DEEPNORM_BETA)
    ffn_w_in = nrm((N_EVEN, D, 2 * FFN_DENSE), D ** -0.5)
    ffn_w_down = nrm((N_EVEN, FFN_DENSE, D), FFN_DENSE ** -0.5 * DEEPNORM_BETA)
    odd_w_qkv = nrm((N_ODD, D, 3 * DIL_W), D ** -0.5)
    odd_w_out = nrm((N_ODD, DIL_W, D), DIL_W ** -0.5 * DEEPNORM_BETA)
    moe_router = nrm((N_ODD, D, N_EXPERTS), D ** -0.5)
    moe_w_in = nrm((N_ODD, N_EXPERTS, D, 2 * FFN_EXPERT), D ** -0.5)
    moe_w_down = nrm((N_ODD, N_EXPERTS, FFN_EXPERT, D), FFN_EXPERT ** -0.5 * DEEPNORM_BETA)
    return {"x": x, "rel_bias": rel_bias, "ln_mix_g": ln_mix_g, "ln_mix_b": ln_mix_b,
            "ln_ffn_g": ln_ffn_g, "ln_ffn_b": ln_ffn_b, "even_w_in": even_w_in,
            "even_conv_w": even_conv_w, "even_conv_b": even_conv_b, "even_dt_bias": even_dt_bias,
            "even_A_log": even_A_log, "even_D": even_D, "even_norm_g": even_norm_g,
            "even_w_out": even_w_out, "ffn_w_in": ffn_w_in, "ffn_w_down": ffn_w_down,
            "odd_w_qkv": odd_w_qkv, "odd_w_out": odd_w_out, "moe_router": moe_router,
            "moe_w_in": moe_w_in, "moe_w_down": moe_w_down}


def reference(x, rel_bias, ln_mix_g, ln_mix_b, ln_ffn_g, ln_ffn_b, even_w_in, even_conv_w,
              even_conv_b, even_dt_bias, even_A_log, even_D, even_norm_g, even_w_out,
              ffn_w_in, ffn_w_down, odd_w_qkv, odd_w_out, moe_router, moe_w_in, moe_w_down):
    for l in range(DEPTH):
        e = l // 2
        if l % 2 == 0:
            h = even_mixer(x, even_w_in[e], even_conv_w[e], even_conv_b[e], even_dt_bias[e],
                           even_A_log[e], even_D[e], even_norm_g[e], even_w_out[e], rel_bias)
        else:
            h = odd_mixer(x, odd_w_qkv[e], odd_w_out[e], rel_bias)
        x = layer_norm(DEEPNORM_ALPHA * x + h, ln_mix_g[l], ln_mix_b[l])
        if l % 2 == 0:
            f = swiglu(x, ffn_w_in[e], ffn_w_down[e])
        else:
            f = moe_swiglu(x, moe_router[e], moe_w_in[e], moe_w_down[e])
        x = layer_norm(DEEPNORM_ALPHA * x + f, ln_ffn_g[l], ln_ffn_b[l])
    return x
```

```python
import functools
import math

import jax
import jax.numpy as jnp
from jax import lax
from jax.experimental import pallas as pl
from jax.experimental.pallas import tpu as pltpu

F32 = jnp.float32
BF16 = jnp.bfloat16
HIGHEST = lax.Precision.HIGHEST

LANES = 128
SUBLANES = 8
VMEM_LIMIT_BYTES = 48 * 1024 * 1024

D_MODEL = 1024
HEAD_DIM = 64
MOBA_HEADS = 8
MOBA_BLOCK = 256
MOBA_TOPK = 3
SSM_D_INNER = D_MODEL
SSM_HEADDIM = 64
SSM_HEADS = SSM_D_INNER // SSM_HEADDIM
SSM_GROUPS = 4
SSM_STATE = 128
SSM_CONV = 4
SSM_CHUNK = 128
DIL_HEADS = D_MODEL // HEAD_DIM
DIL_BRANCHES = ((128, 1), (512, 4), (2048, 16))
DIL_BLOCK = 128
REL_BUCKETS = 32
REL_MAX_DIST = 2048
REL_HEADS = max(MOBA_HEADS, DIL_HEADS)
FFN_DENSE = 2816
N_EXPERTS = 8
MOE_TOPK = 2
FFN_EXPERT = 3584
LN_EPS = 1e-5
RMS_EPS = 1e-5
MOBA_W = MOBA_HEADS * HEAD_DIM
SSM_BC = SSM_GROUPS * SSM_STATE
SSM_XBC = SSM_D_INNER + 2 * SSM_BC
EVEN_MAIN = 3 * MOBA_W + SSM_D_INNER + SSM_XBC
GROUP_W = SSM_D_INNER // SSM_GROUPS

NEG = -1e30
PAIR = LANES // HEAD_DIM


def _cparams(sem):
    return pltpu.CompilerParams(dimension_semantics=sem, vmem_limit_bytes=VMEM_LIMIT_BYTES)


def _dot(a, b, **kw):
    return jnp.dot(a, b, preferred_element_type=F32, **kw)


def _dot_nt(a, b):
    return lax.dot_general(a, b, (((1,), (1,)), ((), ())), preferred_element_type=F32)


def _silu(x):
    return x * jax.nn.sigmoid(x)


def _layer_norm(y, g, b):
    mu = jnp.mean(y, axis=-1, keepdims=True)
    yc = y - mu
    var = jnp.mean(yc * yc, axis=-1, keepdims=True)
    return yc * lax.rsqrt(var + LN_EPS) * g + b


def _bias_tile_kernel(rb_ref, idx_ref, o_ref):
    h = pl.program_id(0)
    idx = idx_ref[0]
    acc = jnp.zeros(idx.shape, F32)
    for b in range(REL_BUCKETS):
        acc = jnp.where(idx == b, rb_ref[b, h], acc)
    o_ref[0, 0] = jnp.where(idx < 0, NEG, acc)


def _bias_tiles(rel_bias, idx, heads):
    n, r, c = idx.shape
    return pl.pallas_call(
        _bias_tile_kernel,
        grid=(heads, n),
        in_specs=[pl.BlockSpec(memory_space=pltpu.SMEM),
                  pl.BlockSpec((1, r, c), lambda h, i: (i, 0, 0))],
        out_specs=pl.BlockSpec((1, 1, r, c), lambda h, i: (h, i, 0, 0)),
        out_shape=jax.ShapeDtypeStruct((heads, n, r, c), F32),
        compiler_params=_cparams(("arbitrary", "arbitrary")),
        name="bias_tiles",
    )(rel_bias, idx)


def _rel_bucket(dist):
    max_exact = REL_BUCKETS // 2
    d = jnp.maximum(dist, 0)
    df = jnp.maximum(d, 1).astype(F32)
    large = max_exact + (jnp.log(df / max_exact) / math.log(REL_MAX_DIST / max_exact)
                         * (REL_BUCKETS - max_exact)).astype(jnp.int32)
    large = jnp.minimum(large, REL_BUCKETS - 1)
    return jnp.where(d < max_exact, d, large)


def _moba_bucket_idx(nblk):
    i = jnp.arange(MOBA_BLOCK, dtype=jnp.int32)[:, None]
    j = jnp.arange(MOBA_BLOCK, dtype=jnp.int32)[None, :]
    delta = jnp.arange(nblk, dtype=jnp.int32)[:, None, None]
    return _rel_bucket(delta * MOBA_BLOCK + i - j)


def _dil_bucket_idx():
    qi = jnp.arange(DIL_BLOCK, dtype=jnp.int32)[:, None]
    ki = jnp.arange(2 * DIL_BLOCK, dtype=jnp.int32)[None, :]
    steps = qi + DIL_BLOCK - ki
    tiles = []
    for window, dil in DIL_BRANCHES:
        span = window // dil
        valid = (steps >= 0) & (steps <= span)
        tiles.append(jnp.where(valid, _rel_bucket(steps * dil), -1))
    return jnp.stack(tiles)


def _proj_kernel(x_ref, w_ref, *o_refs, segs):
    xb = x_ref[...].astype(BF16)
    for o_ref, (c0, c1) in zip(o_refs, segs):
        o_ref[...] = _dot(xb, w_ref[:, c0:c1]).astype(o_ref.dtype)


def _project(x2, w_bf16, segs, dtypes, tm):
    t, d = x2.shape
    n = w_bf16.shape[1]
    return pl.pallas_call(
        functools.partial(_proj_kernel, segs=segs),
        grid=(t // tm,),
        in_specs=[pl.BlockSpec((tm, d), lambda i: (i, 0)),
                  pl.BlockSpec((d, n), lambda i: (0, 0))],
        out_specs=[pl.BlockSpec((tm, c1 - c0), lambda i: (i, 0)) for c0, c1 in segs],
        out_shape=[jax.ShapeDtypeStruct((t, c1 - c0), dt) for (c0, c1), dt in zip(segs, dtypes)],
        compiler_params=_cparams(("parallel",)),
        name="in_proj",
    )(x2, w_bf16)


def _moba_kernel(q_ref, k_ref, v_ref, bias_ref, o_ref, kmean_ref, *, nblk):
    qi = pl.program_id(2)
    blk = MOBA_BLOCK
    seq = k_ref.shape[1]
    lane = lax.broadcasted_iota(jnp.int32, (1, LANES), 1)

    @pl.when(qi == 0)
    def _():
        r = lax.broadcasted_iota(jnp.int32, (nblk, seq), 0)
        c = lax.broadcasted_iota(jnp.int32, (nblk, seq), 1)
        avg = jnp.where((c >= r * blk) & (c < (r + 1) * blk), 1.0 / blk, 0.0).astype(BF16)
        kmean_ref[...] = _dot(avg, k_ref[0])

    q = q_ref[0]
    kmean = kmean_ref[...]
    jf = lax.broadcasted_iota(jnp.int32, (blk, nblk), 1).astype(F32)
    qif = qi.astype(F32)
    row = lax.broadcasted_iota(jnp.int32, (blk, blk), 0)
    col = lax.broadcasted_iota(jnp.int32, (blk, blk), 1)
    erow = lax.broadcasted_iota(jnp.int32, (nblk, blk), 0)
    own0 = pl.multiple_of(qi * blk, blk)
    k_own = k_ref[0, pl.ds(own0, blk), :]
    v_own = v_ref[0, pl.ds(own0, blk), :]

    outs = []
    for hh in range(PAIR):
        in_head = (lane >= hh * HEAD_DIM) & (lane < (hh + 1) * HEAD_DIM)
        km = jnp.where(in_head, kmean, 0.0)
        km_hi = km.astype(BF16)
        km_lo = (km - km_hi.astype(F32)).astype(BF16)
        gate = _dot_nt(q, km_hi) + _dot_nt(q, km_lo)
        gate = jnp.where(jf < qif, gate, -jnp.inf)
        sel = jnp.zeros((blk, nblk), F32)
        for _ in range(MOBA_TOPK):
            best = jnp.max(gate, axis=1, keepdims=True)
            first = jnp.min(jnp.where(gate == best, jf, float(nblk)), axis=1, keepdims=True)
            hit = jf == first
            sel = jnp.where(hit & (best > -jnp.inf), 1.0, sel)
            gate = jnp.where(hit, -jnp.inf, gate)
        sel_b = sel.astype(BF16)

        qh = jnp.where(in_head, q, jnp.zeros_like(q)) * (HEAD_DIM ** -0.5)
        s = _dot_nt(qh, k_own) + bias_ref[hh, 0]
        s = jnp.where(row >= col, s, NEG)
        m0 = jnp.max(s, axis=1, keepdims=True)
        p = jnp.exp(s - m0)
        l0 = jnp.sum(p, axis=1, keepdims=True)
        acc0 = _dot(p.astype(BF16), v_own)

        def body(j, carry, qh=qh, sel_b=sel_b, hh=hh):
            m, l, acc = carry
            r0 = pl.multiple_of(j * blk, blk)
            kj = k_ref[0, pl.ds(r0, blk), :]
            vj = v_ref[0, pl.ds(r0, blk), :]
            s = _dot_nt(qh, kj) + bias_ref[hh, qi - j]
            chosen = _dot(sel_b, jnp.where(erow == j, 1.0, 0.0).astype(BF16))
            s = jnp.where(chosen > 0.5, s, NEG)
            m_new = jnp.maximum(m, jnp.max(s, axis=1, keepdims=True))
            alpha = jnp.exp(m - m_new)
            p = jnp.exp(s - m_new)
            l = alpha * l + jnp.sum(p, axis=1, keepdims=True)
            acc = alpha * acc + _dot(p.astype(BF16), vj)
            return m_new, l, acc

        _, l_f, acc_f = lax.fori_loop(0, qi, body, (m0, l0, acc0))
        outs.append(acc_f / l_f)
    o_ref[0] = jnp.where(lane < HEAD_DIM, outs[0], outs[1]).astype(o_ref.dtype)


def _moba_attention(q, k, v, bias):
    b, s, w = q.shape
    nblk = s // MOBA_BLOCK
    npair = w // LANES
    return pl.pallas_call(
        functools.partial(_moba_kernel, nblk=nblk),
        grid=(npair, b, nblk),
        in_specs=[pl.BlockSpec((1, MOBA_BLOCK, LANES), lambda p, i, j: (i, j, p)),
                  pl.BlockSpec((1, s, LANES), lambda p, i, j: (i, 0, p)),
                  pl.BlockSpec((1, s, LANES), lambda p, i, j: (i, 0, p)),
                  pl.BlockSpec((PAIR, nblk, MOBA_BLOCK, MOBA_BLOCK), lambda p, i, j: (p, 0, 0, 0))],
        out_specs=pl.BlockSpec((1, MOBA_BLOCK, LANES), lambda p, i, j: (i, j, p)),
        out_shape=jax.ShapeDtypeStruct((b, s, w), BF16),
        scratch_shapes=[pltpu.VMEM((nblk, LANES), F32)],
        compiler_params=_cparams(("arbitrary", "arbitrary", "arbitrary")),
        name="moba_attention",
    )(q, k, v, bias)


def _ssd_kernel(xbc_ref, z_ref, dt_ref, cw_ref, cb_ref, dtb_ref, alog_ref, dx_ref, ng_ref,
                o_ref, ext_ref, h_ref, y_ref):
    c = pl.program_id(1)
    L = SSM_CHUNK
    tail = SUBLANES

    @pl.when(c == 0)
    def _():
        ext_ref[0:tail, :] = jnp.zeros((tail, SSM_XBC), F32)
        h_ref[...] = jnp.zeros(h_ref.shape, F32)

    u = xbc_ref[0]
    ext_ref[tail:tail + L, :] = u
    conv = u * cw_ref[SSM_CONV - 1:SSM_CONV, :] + cb_ref[...]
    for j in range(1, SSM_CONV):
        conv = conv + ext_ref[tail - j:tail - j + L, :] * cw_ref[SSM_CONV - 1 - j:SSM_CONV - j, :]
    ext_ref[0:tail, :] = u[L - tail:L, :]
    xc = _silu(conv)
    xs = xc[:, :SSM_D_INNER]

    x_dt = dt_ref[0] + dtb_ref[...]
    dt = jnp.maximum(x_dt, 0.0) + jnp.log1p(jnp.exp(-jnp.abs(x_dt)))
    a = dt * (-jnp.exp(alog_ref[...]))
    row = lax.broadcasted_iota(jnp.int32, (L, L), 0)
    col = lax.broadcasted_iota(jnp.int32, (L, L), 1)
    causal = row >= col
    a_cs = _dot(jnp.where(causal, 1.0, 0.0), a, precision=HIGHEST)
    a_cs_t = a_cs.T
    er = lax.broadcasted_iota(jnp.int32, (LANES, SSM_D_INNER), 0)
    ec = lax.broadcasted_iota(jnp.int32, (LANES, SSM_D_INNER), 1)
    expand = jnp.where((ec >= er * SSM_HEADDIM) & (ec < (er + 1) * SSM_HEADDIM), 1.0, 0.0)
    a_cs_x = _dot(a_cs, expand, precision=HIGHEST)
    dt_x = _dot(dt, expand, precision=HIGHEST)
    a_last_x = a_cs_x[L - 1:L, :]
    xdt = xs * dt_x
    w_end = xdt * jnp.exp(a_last_x - a_cs_x)
    decay_in = jnp.exp(a_cs_x)
    chunk_decay = jnp.exp(a_last_x)
    lane = lax.broadcasted_iota(jnp.int32, (1, LANES), 1)
    hpg = SSM_HEADS // SSM_GROUPS

    for g in range(SSM_GROUPS):
        gs = slice(g * GROUP_W, (g + 1) * GROUP_W)
        b_g = xc[:, SSM_D_INNER + g * SSM_STATE:SSM_D_INNER + (g + 1) * SSM_STATE]
        c_g = xc[:, SSM_D_INNER + SSM_BC + g * SSM_STATE:SSM_D_INNER + SSM_BC + (g + 1) * SSM_STATE]
        b_bf = b_g.astype(BF16)
        c_bf = c_g.astype(BF16)
        cb = _dot_nt(c_bf, b_bf)
        h_in = h_ref[g]
        y_off = _dot(c_bf, h_in.astype(BF16)) * decay_in[:, gs]
        st = _dot(b_g.T.astype(BF16), w_end[:, gs].astype(BF16))
        h_ref[g] = h_in * chunk_decay[:, gs] + st
        for pr in range(hpg // PAIR):
            ps = slice(g * GROUP_W + pr * LANES, g * GROUP_W + (pr + 1) * LANES)
            x_pair = xdt[:, ps].astype(BF16)
            ys = []
            for hh in range(PAIR):
                h = g * hpg + pr * PAIR + hh
                seg = a_cs[:, h:h + 1] - a_cs_t[h:h + 1, :]
                decay = jnp.exp(jnp.where(causal, seg, -jnp.inf))
                ys.append(_dot((cb * decay).astype(BF16), x_pair))
            y_diag = jnp.where(lane < SSM_HEADDIM, ys[0], ys[1])
            y_ref[:, ps] = y_diag + y_off[:, pr * LANES:(pr + 1) * LANES] + dx_ref[:, ps] * xs[:, ps]

    yz = y_ref[...] * _silu(z_ref[0])
    for g in range(SSM_GROUPS):
        gs = slice(g * GROUP_W, (g + 1) * GROUP_W)
        blk = yz[:, gs]
        ms = jnp.mean(blk * blk, axis=-1, keepdims=True)
        o_ref[0, :, gs] = (blk * lax.rsqrt(ms + RMS_EPS) * ng_ref[:, gs]).astype(o_ref.dtype)


def _ssd_mixer(xbc, z, dt_raw, conv_w, conv_b, dt_bias, a_log, d_skip, norm_g):
    b, s, _ = xbc.shape
    L = SSM_CHUNK
    pad = LANES - SSM_HEADS
    dtb = jnp.pad(dt_bias, (0, pad)).reshape(1, LANES)
    alog = jnp.pad(a_log, (0, pad)).reshape(1, LANES)
    d_x = jnp.repeat(d_skip, SSM_HEADDIM).reshape(1, SSM_D_INNER)
    row = lambda n: pl.BlockSpec((1, n), lambda i, j: (0, 0))
    return pl.pallas_call(
        _ssd_kernel,
        grid=(b, s // L),
        in_specs=[pl.BlockSpec((1, L, SSM_XBC), lambda i, j: (i, j, 0)),
                  pl.BlockSpec((1, L, SSM_D_INNER), lambda i, j: (i, j, 0)),
                  pl.BlockSpec((1, L, LANES), lambda i, j: (i, j, 0)),
                  pl.BlockSpec((SSM_CONV, SSM_XBC), lambda i, j: (0, 0)),
                  row(SSM_XBC), row(LANES), row(LANES), row(SSM_D_INNER), row(SSM_D_INNER)],
        out_specs=pl.BlockSpec((1, L, SSM_D_INNER), lambda i, j: (i, j, 0)),
        out_shape=jax.ShapeDtypeStruct((b, s, SSM_D_INNER), BF16),
        scratch_shapes=[pltpu.VMEM((SUBLANES + L, SSM_XBC), F32),
                        pltpu.VMEM((SSM_GROUPS, SSM_STATE, GROUP_W), F32),
                        pltpu.VMEM((L, SSM_D_INNER), F32)],
        compiler_params=_cparams(("arbitrary", "arbitrary")),
        name="ssd_mixer",
    )(xbc, z, dt_raw, conv_w, conv_b.reshape(1, SSM_XBC), dtb, alog, d_x,
      norm_g.reshape(1, SSM_D_INNER))


def _dil_kernel(q_ref, k_ref, v_ref, bias_ref, o_ref, lse_ref, *, nb):
    blk = DIL_BLOCK
    lane = lax.broadcasted_iota(jnp.int32, (1, LANES), 1)

    def block(n, carry):
        r0 = pl.multiple_of(n * blk, blk)
        rp = pl.multiple_of(jnp.maximum(n - 1, 0) * blk, blk)
        q = q_ref[0, pl.ds(r0, blk), :]
        k_cur = k_ref[0, pl.ds(r0, blk), :]
        v_cur = v_ref[0, pl.ds(r0, blk), :]
        k_prev = k_ref[0, pl.ds(rp, blk), :]
        v_prev = v_ref[0, pl.ds(rp, blk), :]
        outs, lses = [], []
        for hh in range(PAIR):
            in_head = (lane >= hh * HEAD_DIM) & (lane < (hh + 1) * HEAD_DIM)
            qh = jnp.where(in_head, q, jnp.zeros_like(q)) * (HEAD_DIM ** -0.5)
            s_prev = _dot_nt(qh, k_prev) + bias_ref[hh, 0, :, 0:blk]
            s_prev = jnp.where(n > 0, s_prev, NEG)
            s_cur = _dot_nt(qh, k_cur) + bias_ref[hh, 0, :, blk:2 * blk]
            m = jnp.maximum(jnp.max(s_prev, axis=1, keepdims=True), jnp.max(s_cur, axis=1, keepdims=True))
            e_prev = jnp.exp(s_prev - m)
            e_cur = jnp.exp(s_cur - m)
            den = jnp.sum(e_prev, axis=1, keepdims=True) + jnp.sum(e_cur, axis=1, keepdims=True)
            o = (_dot(e_prev.astype(BF16), v_prev) + _dot(e_cur.astype(BF16), v_cur)) / den
            outs.append(o)
            lses.append(m + jnp.log(den))
        o_ref[0, pl.ds(r0, blk), :] = jnp.where(lane < HEAD_DIM, outs[0], outs[1])
        lse_ref[0, pl.ds(r0, blk), :] = jnp.where(lane < HEAD_DIM, lses[0], lses[1])
        return carry

    lax.fori_loop(0, nb, block, 0)


def _dil_branch(q, k, v, bias, branch, dil):
    b, s, w = q.shape
    L = s // dil
    nb = L // DIL_BLOCK
    npair = w // LANES
    view = lambda a: a.reshape(b, L, dil * w)
    spec = pl.BlockSpec((1, L, LANES), lambda i, r, p: (i, 0, r * npair + p))
    o, lse = pl.pallas_call(
        functools.partial(_dil_kernel, nb=nb),
        grid=(b, dil, npair),
        in_specs=[spec, spec, spec,
                  pl.BlockSpec((PAIR, 1, DIL_BLOCK, 2 * DIL_BLOCK), lambda i, r, p: (p, branch, 0, 0))],
        out_specs=[spec, spec],
        out_shape=[jax.ShapeDtypeStruct((b, L, dil * w), F32)] * 2,
        compiler_params=_cparams(("arbitrary", "arbitrary", "arbitrary")),
        name=f"dilated_branch{branch}",
    )(view(q), view(k), view(v), bias)
    return o.reshape(b * s, w), lse.reshape(b * s, w)


def _dil_merge_kernel(*refs):
    n = (len(refs) - 1) // 2
    o_refs, l_refs, out_ref = refs[:n], refs[n:2 * n], refs[2 * n]
    lses = [r[...] for r in l_refs]
    m = functools.reduce(jnp.maximum, lses)
    ws = [jnp.exp(l - m) for l in lses]
    num = sum(w * r[...] for w, r in zip(ws, o_refs))
    out_ref[...] = (num / sum(ws)).astype(out_ref.dtype)


def _dil_merge(outs, lses, tm):
    t, w = outs[0].shape
    spec = pl.BlockSpec((tm, w), lambda i: (i, 0))
    return pl.pallas_call(
        _dil_merge_kernel,
        grid=(t // tm,),
        in_specs=[spec] * (2 * len(outs)),
        out_specs=spec,
        out_shape=jax.ShapeDtypeStruct((t, w), BF16),
        compiler_params=_cparams(("parallel",)),
        name="dilated_merge",
    )(*outs, *lses)


def _outproj_ln_kernel(*refs, n_in, alpha):
    a_refs, w_refs = refs[:n_in], refs[n_in:2 * n_in]
    x_ref, g_ref, b_ref, o_ref = refs[2 * n_in:]
    acc = _dot(a_refs[0][...], w_refs[0][...])
    for a_ref, w_ref in zip(a_refs[1:], w_refs[1:]):
        acc = acc + _dot(a_ref[...], w_ref[...])
    o_ref[...] = _layer_norm(alpha * x_ref[...] + acc, g_ref[...], b_ref[...])


def _outproj_ln(acts, weights, x2, g, b, alpha, tm):
    t, d = x2.shape
    n_in = len(acts)
    in_specs = ([pl.BlockSpec((tm, a.shape[1]), lambda i: (i, 0)) for a in acts]
                + [pl.BlockSpec(w.shape, lambda i: (0, 0)) for w in weights]
                + [pl.BlockSpec((tm, d), lambda i: (i, 0)),
                   pl.BlockSpec((1, d), lambda i: (0, 0)), pl.BlockSpec((1, d), lambda i: (0, 0))])
    return pl.pallas_call(
        functools.partial(_outproj_ln_kernel, n_in=n_in, alpha=alpha),
        grid=(t // tm,),
        in_specs=in_specs,
        out_specs=pl.BlockSpec((tm, d), lambda i: (i, 0)),
        out_shape=jax.ShapeDtypeStruct((t, d), F32),
        compiler_params=_cparams(("parallel",)),
        name="out_proj_ln",
    )(*acts, *weights, x2, g.reshape(1, d), b.reshape(1, d))


def _ffn_kernel(x_ref, wg_ref, wu_ref, wd_ref, g_ref, b_ref, o_ref, acc_ref, *, alpha):
    f = pl.program_id(1)

    @pl.when(f == 0)
    def _():
        acc_ref[...] = jnp.zeros(acc_ref.shape, F32)

    xb = x_ref[...].astype(BF16)
    gate = _dot(xb, wg_ref[...])
    up = _dot(xb, wu_ref[...])
    acc_ref[...] += _dot((_silu(gate) * up).astype(BF16), wd_ref[...])

    @pl.when(f == pl.num_programs(1) - 1)
    def _():
        o_ref[...] = _layer_norm(alpha * x_ref[...] + acc_ref[...], g_ref[...], b_ref[...])


def _ffn_ln(x2, w_in, w_down, g, b, alpha, tm, tf):
    t, d = x2.shape
    ff = w_down.shape[0]
    nf = ff // tf
    return pl.pallas_call(
        functools.partial(_ffn_kernel, alpha=alpha),
        grid=(t // tm, nf),
        in_specs=[pl.BlockSpec((tm, d), lambda i, f: (i, 0)),
                  pl.BlockSpec((d, tf), lambda i, f: (0, f)),
                  pl.BlockSpec((d, tf), lambda i, f: (0, f + nf)),
                  pl.BlockSpec((tf, d), lambda i, f: (f, 0)),
                  pl.BlockSpec((1, d), lambda i, f: (0, 0)),
                  pl.BlockSpec((1, d), lambda i, f: (0, 0))],
        out_specs=pl.BlockSpec((tm, d), lambda i, f: (i, 0)),
        out_shape=jax.ShapeDtypeStruct((t, d), F32),
        scratch_shapes=[pltpu.VMEM((tm, d), F32)],
        compiler_params=_cparams(("parallel", "arbitrary")),
        name="ffn_ln",
    )(x2, w_in, w_in, w_down, g.reshape(1, d), b.reshape(1, d))


def _router_kernel(x_ref, wr_ref, c_ref):
    logits = _dot(x_ref[...], wr_ref[...], precision=HIGHEST)
    lanef = lax.broadcasted_iota(jnp.int32, logits.shape, 1).astype(F32)
    cur = jnp.where(lanef < N_EXPERTS, logits, -jnp.inf)
    vals, hits = [], []
    for _ in range(MOE_TOPK):
        best = jnp.max(cur, axis=1, keepdims=True)
        first = jnp.min(jnp.where(cur == best, lanef, float(LANES)), axis=1, keepdims=True)
        hit = lanef == first
        vals.append(best)
        hits.append(hit)
        cur = jnp.where(hit, -jnp.inf, cur)
    es = [jnp.exp(v - vals[0]) for v in vals]
    den = sum(es)
    comb = jnp.zeros(logits.shape, F32)
    for e, hit in zip(es, hits):
        comb = jnp.where(hit, e / den, comb)
    c_ref[...] = comb


def _router(x2, w_router, tm):
    t, d = x2.shape
    wr = jnp.pad(w_router, ((0, 0), (0, LANES - N_EXPERTS)))
    return pl.pallas_call(
        _router_kernel,
        grid=(t // tm,),
        in_specs=[pl.BlockSpec((tm, d), lambda i: (i, 0)),
                  pl.BlockSpec((d, LANES), lambda i: (0, 0))],
        out_specs=pl.BlockSpec((tm, LANES), lambda i: (i, 0)),
        out_shape=jax.ShapeDtypeStruct((t, LANES), F32),
        compiler_params=_cparams(("parallel",)),
        name="moe_router",
    )(x2, wr)


def _moe_kernel(x_ref, c_ref, wg_ref, wu_ref, wd_ref, g_ref, b_ref, o_ref, acc_ref, *, alpha):
    e = pl.program_id(1)
    f = pl.program_id(2)

    @pl.when((e == 0) & (f == 0))
    def _():
        acc_ref[...] = jnp.zeros(acc_ref.shape, F32)

    xb = x_ref[...].astype(BF16)
    gate = _dot(xb, wg_ref[0])
    up = _dot(xb, wu_ref[0])
    comb = c_ref[...]
    lane = lax.broadcasted_iota(jnp.int32, comb.shape, 1)
    ce = jnp.sum(jnp.where(lane == e, comb, 0.0), axis=1, keepdims=True)
    acc_ref[...] += _dot((_silu(gate) * up * ce).astype(BF16), wd_ref[0])

    @pl.when((e == pl.num_programs(1) - 1) & (f == pl.num_programs(2) - 1))
    def _():
        o_ref[...] = _layer_norm(alpha * x_ref[...] + acc_ref[...], g_ref[...], b_ref[...])


def _moe_ln(x2, comb, w_in, w_down, g, b, alpha, tm, tf):
    t, d = x2.shape
    ne, ff, _ = w_down.shape
    nf = ff // tf
    return pl.pallas_call(
        functools.partial(_moe_kernel, alpha=alpha),
        grid=(t // tm, ne, nf),
        in_specs=[pl.BlockSpec((tm, d), lambda i, e, f: (i, 0)),
                  pl.BlockSpec((tm, LANES), lambda i, e, f: (i, 0)),
                  pl.BlockSpec((1, d, tf), lambda i, e, f: (e, 0, f)),
                  pl.BlockSpec((1, d, tf), lambda i, e, f: (e, 0, f + nf)),
                  pl.BlockSpec((1, tf, d), lambda i, e, f: (e, f, 0)),
                  pl.BlockSpec((1, d), lambda i, e, f: (0, 0)),
                  pl.BlockSpec((1, d), lambda i, e, f: (0, 0))],
        out_specs=pl.BlockSpec((tm, d), lambda i, e, f: (i, 0)),
        out_shape=jax.ShapeDtypeStruct((t, d), F32),
        scratch_shapes=[pltpu.VMEM((tm, d), F32)],
        compiler_params=_cparams(("parallel", "arbitrary", "arbitrary")),
        name="moe_ln",
    )(x2, comb, w_in, w_in, w_down, g.reshape(1, d), b.reshape(1, d))


def _tile(t, want):
    while t % want:
        want //= 2
    return want


def _even_layer(x2, bsz, seq, w_in, conv_w, conv_b, dt_bias, a_log, d_skip, norm_g, w_out,
                moba_bias, ln_g, ln_b, alpha):
    t = x2.shape[0]
    w_cat = jnp.concatenate(
        [w_in[:, :EVEN_MAIN], jnp.pad(w_in[:, EVEN_MAIN:], ((0, 0), (0, LANES - SSM_HEADS)))], axis=1
    ).astype(BF16)
    bounds = (0, MOBA_W, 2 * MOBA_W, 3 * MOBA_W, 3 * MOBA_W + SSM_D_INNER, EVEN_MAIN, EVEN_MAIN + LANES)
    segs = tuple(zip(bounds[:-1], bounds[1:]))
    q, k, v, z, xbc, dt_raw = _project(x2, w_cat, segs, (BF16, BF16, BF16, F32, F32, F32), _tile(t, 256))
    r3 = lambda a: a.reshape(bsz, seq, a.shape[-1])
    attn = _moba_attention(r3(q), r3(k), r3(v), moba_bias).reshape(t, MOBA_W)
    ssm = _ssd_mixer(r3(xbc), r3(z), r3(dt_raw), conv_w, conv_b, dt_bias, a_log, d_skip,
                     norm_g).reshape(t, SSM_D_INNER)
    w_o = w_out.astype(BF16)
    return _outproj_ln([attn, ssm], [w_o[:MOBA_W], w_o[MOBA_W:]], x2, ln_g, ln_b, alpha, _tile(t, 512))


def _odd_layer(x2, bsz, seq, w_qkv, w_out, dil_bias, ln_g, ln_b, alpha):
    t = x2.shape[0]
    w = DIL_HEADS * HEAD_DIM
    segs = ((0, w), (w, 2 * w), (2 * w, 3 * w))
    q, k, v = _project(x2, w_qkv.astype(BF16), segs, (BF16, BF16, BF16), _tile(t, 256))
    r3 = lambda a: a.reshape(bsz, seq, w)
    outs, lses = [], []
    for branch, (_, dil) in enumerate(DIL_BRANCHES):
        o, lse = _dil_branch(r3(q), r3(k), r3(v), dil_bias, branch, dil)
        outs.append(o)
        lses.append(lse)
    merged = _dil_merge(outs, lses, _tile(t, 512))
    return _outproj_ln([merged], [w_out.astype(BF16)], x2, ln_g, ln_b, alpha, _tile(t, 512))


def kernel(x, rel_bias, ln_mix_g, ln_mix_b, ln_ffn_g, ln_ffn_b, even_w_in, even_conv_w, even_conv_b,
           even_dt_bias, even_A_log, even_D, even_norm_g, even_w_out, ffn_w_in, ffn_w_down,
           odd_w_qkv, odd_w_out, moe_router, moe_w_in, moe_w_down):
    bsz, seq, d = x.shape
    depth = ln_mix_g.shape[0]
    alpha = (2.0 * depth) ** 0.25
    t = bsz * seq
    x2 = x.reshape(t, d)
    moba_bias = _bias_tiles(rel_bias, _moba_bucket_idx(seq // MOBA_BLOCK), MOBA_HEADS)
    dil_bias = _bias_tiles(rel_bias, _dil_bucket_idx(), DIL_HEADS)
    for l in range(depth):
        e = l // 2
        if l % 2 == 0:
            x2 = _even_layer(x2, bsz, seq, even_w_in[e], even_conv_w[e], even_conv_b[e], even_dt_bias[e],
                             even_A_log[e], even_D[e], even_norm_g[e], even_w_out[e], moba_bias,
                             ln_mix_g[l], ln_mix_b[l], alpha)
            x2 = _ffn_ln(x2, ffn_w_in[e].astype(BF16), ffn_w_down[e].astype(BF16), ln_ffn_g[l], ln_ffn_b[l],
                         alpha, _tile(t, 512), FFN_DENSE // 2)
        else:
            x2 = _odd_layer(x2, bsz, seq, odd_w_qkv[e], odd_w_out[e], dil_bias, ln_mix_g[l], ln_mix_b[l], alpha)
            comb = _router(x2, moe_router[e], _tile(t, 512))
            x2 = _moe_ln(x2, comb, moe_w_in[e].astype(BF16), moe_w_down[e].astype(BF16), ln_ffn_g[l],
                         ln_ffn_b[l], alpha, _tile(t, 1024), FFN_EXPERT // 4)
    return x2.reshape(bsz, seq, d)
```

```python
import functools
import math

import jax
import jax.numpy as jnp
from jax import lax
from jax.experimental import pallas as pl
from jax.experimental.pallas import tpu as pltpu

F32 = jnp.float32
BF16 = jnp.bfloat16
HIGHEST = lax.Precision.HIGHEST

LANES = 128
SUBLANES = 8
VMEM_LIMIT_BYTES = 48 * 1024 * 1024
MOE_VMEM_LIMIT_BYTES = 54 * 1024 * 1024

D_MODEL = 1024
HEAD_DIM = 64
MOBA_HEADS = 8
MOBA_BLOCK = 256
MOBA_TOPK = 3
MOBA_UNROLL = 2
SSM_D_INNER = D_MODEL
SSM_HEADDIM = 64
SSM_HEADS = SSM_D_INNER // SSM_HEADDIM
SSM_GROUPS = 4
SSM_STATE = 128
SSM_CONV = 4
SSM_CHUNK = 128
DIL_HEADS = D_MODEL // HEAD_DIM
DIL_BRANCHES = ((128, 1), (512, 4), (2048, 16))
DIL_BLOCK = 128
DIL_UNROLL = 2
REL_BUCKETS = 32
REL_MAX_DIST = 2048
REL_HEADS = max(MOBA_HEADS, DIL_HEADS)
FFN_DENSE = 2816
N_EXPERTS = 8
MOE_TOPK = 2
FFN_EXPERT = 3584
LN_EPS = 1e-5
RMS_EPS = 1e-5
MOBA_W = MOBA_HEADS * HEAD_DIM
SSM_BC = SSM_GROUPS * SSM_STATE
SSM_XBC = SSM_D_INNER + 2 * SSM_BC
EVEN_MAIN = 3 * MOBA_W + SSM_D_INNER + SSM_XBC
GROUP_W = SSM_D_INNER // SSM_GROUPS

NEG = -1e30
PAIR = LANES // HEAD_DIM


def _cparams(sem):
    return pltpu.CompilerParams(dimension_semantics=sem, vmem_limit_bytes=VMEM_LIMIT_BYTES)


def _dot(a, b, **kw):
    return jnp.dot(a, b, preferred_element_type=F32, **kw)


def _dot_nt(a, b):
    return lax.dot_general(a, b, (((1,), (1,)), ((), ())), preferred_element_type=F32)


def _silu(x):
    return x * jax.nn.sigmoid(x)


def _layer_norm(y, g, b):
    mu = jnp.mean(y, axis=-1, keepdims=True)
    yc = y - mu
    var = jnp.mean(yc * yc, axis=-1, keepdims=True)
    return yc * lax.rsqrt(var + LN_EPS) * g + b


def _bias_tile_kernel(rb_ref, idx_ref, o_ref):
    h = pl.program_id(0)
    idx = idx_ref[0]
    acc = jnp.zeros(idx.shape, F32)
    for b in range(REL_BUCKETS):
        acc = jnp.where(idx == b, rb_ref[b, h], acc)
    o_ref[0, 0] = jnp.where(idx < 0, NEG, acc)


def _bias_tiles(rel_bias, idx, heads):
    n, r, c = idx.shape
    return pl.pallas_call(
        _bias_tile_kernel,
        grid=(heads, n),
        in_specs=[pl.BlockSpec(memory_space=pltpu.SMEM),
                  pl.BlockSpec((1, r, c), lambda h, i: (i, 0, 0))],
        out_specs=pl.BlockSpec((1, 1, r, c), lambda h, i: (h, i, 0, 0)),
        out_shape=jax.ShapeDtypeStruct((heads, n, r, c), F32),
        compiler_params=_cparams(("arbitrary", "arbitrary")),
        name="bias_tiles",
    )(rel_bias, idx)


def _rel_bucket(dist):
    max_exact = REL_BUCKETS // 2
    d = jnp.maximum(dist, 0)
    df = jnp.maximum(d, 1).astype(F32)
    large = max_exact + (jnp.log(df / max_exact) / math.log(REL_MAX_DIST / max_exact)
                         * (REL_BUCKETS - max_exact)).astype(jnp.int32)
    large = jnp.minimum(large, REL_BUCKETS - 1)
    return jnp.where(d < max_exact, d, large)


def _moba_bucket_idx(nblk):
    j = jnp.arange(MOBA_BLOCK, dtype=jnp.int32)[:, None]
    i = jnp.arange(MOBA_BLOCK, dtype=jnp.int32)[None, :]
    delta = jnp.arange(nblk, dtype=jnp.int32)[:, None, None]
    dist = delta * MOBA_BLOCK + i - j
    return jnp.where(dist >= 0, _rel_bucket(dist), -1)


def _dil_bucket_idx():
    ki = jnp.arange(2 * DIL_BLOCK, dtype=jnp.int32)[:, None]
    qi = jnp.arange(DIL_BLOCK, dtype=jnp.int32)[None, :]
    steps = qi + DIL_BLOCK - ki
    tiles = []
    for window, dil in DIL_BRANCHES:
        span = window // dil
        valid = (steps >= 0) & (steps <= span)
        tiles.append(jnp.where(valid, _rel_bucket(steps * dil), -1))
    return jnp.stack(tiles)


def _proj_kernel(x_ref, w_ref, *o_refs, segs):
    xb = x_ref[...].astype(BF16)
    for o_ref, (c0, c1) in zip(o_refs, segs):
        o_ref[...] = _dot(xb, w_ref[:, c0:c1]).astype(o_ref.dtype)


def _project(x2, w_bf16, segs, dtypes, tm):
    t, d = x2.shape
    n = w_bf16.shape[1]
    return pl.pallas_call(
        functools.partial(_proj_kernel, segs=segs),
        grid=(t // tm,),
        in_specs=[pl.BlockSpec((tm, d), lambda i: (i, 0)),
                  pl.BlockSpec((d, n), lambda i: (0, 0))],
        out_specs=[pl.BlockSpec((tm, c1 - c0), lambda i: (i, 0)) for c0, c1 in segs],
        out_shape=[jax.ShapeDtypeStruct((t, c1 - c0), dt) for (c0, c1), dt in zip(segs, dtypes)],
        compiler_params=_cparams(("parallel",)),
        name="in_proj",
    )(x2, w_bf16)


def _moba_kernel(q_ref, k_ref, v_ref, bias_ref, o_ref, kmean_ref, vt_ref, mask_ref, *, nblk):
    qi = pl.program_id(2)
    blk = MOBA_BLOCK
    seq = k_ref.shape[1]
    lane = lax.broadcasted_iota(jnp.int32, (1, LANES), 1)
    sub = lax.broadcasted_iota(jnp.int32, (LANES, 1), 0)

    @pl.when(qi == 0)
    def _():
        r = lax.broadcasted_iota(jnp.int32, (nblk, seq), 0)
        c = lax.broadcasted_iota(jnp.int32, (nblk, seq), 1)
        avg = jnp.where((c >= r * blk) & (c < (r + 1) * blk), 1.0 / blk, 0.0).astype(BF16)
        kmean_ref[...] = _dot(avg, k_ref[0])

        def build(j, carry):
            r0 = pl.multiple_of(j * blk, blk)
            vt = v_ref[0, pl.ds(r0, blk), :].astype(F32).T
            for hh in range(PAIR):
                in_head = (sub >= hh * HEAD_DIM) & (sub < (hh + 1) * HEAD_DIM)
                ones_row = sub == (1 - hh) * HEAD_DIM
                vt_ref[hh, j] = jnp.where(in_head, vt, jnp.where(ones_row, 1.0, 0.0)).astype(BF16)
            return carry

        lax.fori_loop(0, nblk, build, 0)

    q = q_ref[0]
    kmean = kmean_ref[...]
    bf = lax.broadcasted_iota(jnp.int32, (nblk, blk), 0).astype(F32)
    qif = qi.astype(F32)

    qhs = []
    for hh in range(PAIR):
        in_head = (lane >= hh * HEAD_DIM) & (lane < (hh + 1) * HEAD_DIM)
        km = jnp.where(in_head, kmean, 0.0)
        km_hi = km.astype(BF16)
        km_lo = (km - km_hi.astype(F32)).astype(BF16)
        gate = _dot_nt(km_hi, q) + _dot_nt(km_lo, q)
        gate = jnp.where(bf < qif, gate, -jnp.inf)
        sel = jnp.where(bf == qif, 1.0, 0.0)
        for _ in range(MOBA_TOPK):
            best = jnp.max(gate, axis=0, keepdims=True)
            first = jnp.min(jnp.where(gate == best, bf, float(nblk)), axis=0, keepdims=True)
            hit = bf == first
            sel = jnp.where(hit & (best > -jnp.inf), 1.0, sel)
            gate = jnp.where(hit, -jnp.inf, gate)
        mask_ref[hh] = jnp.where(sel > 0.5, 0.0, NEG)
        qhs.append(jnp.where(in_head, q, jnp.zeros_like(q)) * (HEAD_DIM ** -0.5))

    def body(t, carry):
        j0 = t * MOBA_UNROLL
        r0 = pl.multiple_of(j0 * blk, MOBA_UNROLL * blk)
        k_grp = k_ref[0, pl.ds(r0, MOBA_UNROLL * blk), :]
        s_alls = [_dot_nt(k_grp, qhs[hh]) for hh in range(PAIR)]
        soft = []
        for hh in range(PAIR):
            m = carry[2 * hh]
            parts = []
            for u in range(MOBA_UNROLL):
                j = j0 + u
                delta = jnp.maximum(qi - j, 0)
                parts.append(s_alls[hh][u * blk:(u + 1) * blk] + bias_ref[hh, delta]
                             + mask_ref[hh, pl.ds(j, 1), :])
            m_new = functools.reduce(jnp.maximum, [m] + [jnp.max(s, axis=0, keepdims=True) for s in parts])
            soft.append((m_new, jnp.exp(m - m_new), [jnp.exp(s - m_new).astype(BF16) for s in parts]))
        new = []
        for hh in range(PAIR):
            m_new, alpha, ps = soft[hh]
            acc = alpha * carry[2 * hh + 1]
            for u, p in enumerate(ps):
                acc = acc + _dot(vt_ref[hh, j0 + u], p)
            new += [m_new, acc]
        return tuple(new)

    m_init = jnp.full((1, blk), 4 * NEG, F32)
    acc_init = jnp.zeros((LANES, blk), F32)
    trips = (qi + MOBA_UNROLL) // MOBA_UNROLL
    fin = lax.fori_loop(0, trips, body, (m_init, acc_init, m_init, acc_init))
    outs = []
    for hh in range(PAIR):
        acc = fin[2 * hh + 1]
        den = acc[(1 - hh) * HEAD_DIM:(1 - hh) * HEAD_DIM + 1, :]
        outs.append(acc / den)
    o_t = jnp.where(sub < HEAD_DIM, outs[0], outs[1])
    o_ref[0] = o_t.T.astype(o_ref.dtype)


def _moba_attention(q, k, v, bias):
    b, s, w = q.shape
    nblk = s // MOBA_BLOCK
    assert s % MOBA_BLOCK == 0 and nblk % MOBA_UNROLL == 0, s
    npair = w // LANES
    return pl.pallas_call(
        functools.partial(_moba_kernel, nblk=nblk),
        grid=(npair, b, nblk),
        in_specs=[pl.BlockSpec((1, MOBA_BLOCK, LANES), lambda p, i, j: (i, j, p)),
                  pl.BlockSpec((1, s, LANES), lambda p, i, j: (i, 0, p)),
                  pl.BlockSpec((1, s, LANES), lambda p, i, j: (i, 0, p)),
                  pl.BlockSpec((PAIR, nblk, MOBA_BLOCK, MOBA_BLOCK), lambda p, i, j: (p, 0, 0, 0))],
        out_specs=pl.BlockSpec((1, MOBA_BLOCK, LANES), lambda p, i, j: (i, j, p)),
        out_shape=jax.ShapeDtypeStruct((b, s, w), BF16),
        scratch_shapes=[pltpu.VMEM((nblk, LANES), F32),
                        pltpu.VMEM((PAIR, nblk, LANES, MOBA_BLOCK), BF16),
                        pltpu.VMEM((PAIR, nblk, MOBA_BLOCK), F32)],
        compiler_params=_cparams(("arbitrary", "arbitrary", "arbitrary")),
        name="moba_attention",
    )(q, k, v, bias)


def _ssd_kernel(xbc_ref, z_ref, dt_ref, cw_ref, cb_ref, dtb_ref, alog_ref, dx_ref, ng_ref,
                o_ref, ext_ref, h_ref, y_ref):
    c = pl.program_id(1)
    L = SSM_CHUNK
    tail = SUBLANES

    @pl.when(c == 0)
    def _():
        ext_ref[0:tail, :] = jnp.zeros((tail, SSM_XBC), F32)
        h_ref[...] = jnp.zeros(h_ref.shape, F32)

    u = xbc_ref[0]
    ext_ref[tail:tail + L, :] = u
    conv = u * cw_ref[SSM_CONV - 1:SSM_CONV, :] + cb_ref[...]
    for j in range(1, SSM_CONV):
        conv = conv + ext_ref[tail - j:tail - j + L, :] * cw_ref[SSM_CONV - 1 - j:SSM_CONV - j, :]
    ext_ref[0:tail, :] = u[L - tail:L, :]
    xc = _silu(conv)
    xs = xc[:, :SSM_D_INNER]

    x_dt = dt_ref[0] + dtb_ref[...]
    dt = jnp.maximum(x_dt, 0.0) + jnp.log1p(jnp.exp(-jnp.abs(x_dt)))
    a = dt * (-jnp.exp(alog_ref[...]))
    row = lax.broadcasted_iota(jnp.int32, (L, L), 0)
    col = lax.broadcasted_iota(jnp.int32, (L, L), 1)
    causal = row >= col
    a_cs = _dot(jnp.where(causal, 1.0, 0.0), a, precision=HIGHEST)
    a_cs_t = a_cs.T
    er = lax.broadcasted_iota(jnp.int32, (LANES, SSM_D_INNER), 0)
    ec = lax.broadcasted_iota(jnp.int32, (LANES, SSM_D_INNER), 1)
    expand = jnp.where((ec >= er * SSM_HEADDIM) & (ec < (er + 1) * SSM_HEADDIM), 1.0, 0.0)
    a_cs_x = _dot(a_cs, expand, precision=HIGHEST)
    dt_x = _dot(dt, expand, precision=HIGHEST)
    a_last_x = a_cs_x[L - 1:L, :]
    xdt = xs * dt_x
    w_end = xdt * jnp.exp(a_last_x - a_cs_x)
    decay_in = jnp.exp(a_cs_x)
    chunk_decay = jnp.exp(a_last_x)
    lane = lax.broadcasted_iota(jnp.int32, (1, LANES), 1)
    hpg = SSM_HEADS // SSM_GROUPS

    for g in range(SSM_GROUPS):
        gs = slice(g * GROUP_W, (g + 1) * GROUP_W)
        b_g = xc[:, SSM_D_INNER + g * SSM_STATE:SSM_D_INNER + (g + 1) * SSM_STATE]
        c_g = xc[:, SSM_D_INNER + SSM_BC + g * SSM_STATE:SSM_D_INNER + SSM_BC + (g + 1) * SSM_STATE]
        b_bf = b_g.astype(BF16)
        c_bf = c_g.astype(BF16)
        cb = _dot_nt(c_bf, b_bf)
        h_in = h_ref[g]
        y_off = _dot(c_bf, h_in.astype(BF16)) * decay_in[:, gs]
        st = _dot(b_g.T.astype(BF16), w_end[:, gs].astype(BF16))
        h_ref[g] = h_in * chunk_decay[:, gs] + st
        for pr in range(hpg // PAIR):
            ps = slice(g * GROUP_W + pr * LANES, g * GROUP_W + (pr + 1) * LANES)
            x_pair = xdt[:, ps].astype(BF16)
            ys = []
            for hh in range(PAIR):
                h = g * hpg + pr * PAIR + hh
                seg = a_cs[:, h:h + 1] - a_cs_t[h:h + 1, :]
                decay = jnp.exp(jnp.where(causal, seg, -jnp.inf))
                ys.append(_dot((cb * decay).astype(BF16), x_pair))
            y_diag = jnp.where(lane < SSM_HEADDIM, ys[0], ys[1])
            y_ref[:, ps] = y_diag + y_off[:, pr * LANES:(pr + 1) * LANES] + dx_ref[:, ps] * xs[:, ps]

    yz = y_ref[...] * _silu(z_ref[0])
    for g in range(SSM_GROUPS):
        gs = slice(g * GROUP_W, (g + 1) * GROUP_W)
        blk = yz[:, gs]
        ms = jnp.mean(blk * blk, axis=-1, keepdims=True)
        o_ref[0, :, gs] = (blk * lax.rsqrt(ms + RMS_EPS) * ng_ref[:, gs]).astype(o_ref.dtype)


def _ssd_mixer(xbc, z, dt_raw, conv_w, conv_b, dt_bias, a_log, d_skip, norm_g):
    b, s, _ = xbc.shape
    L = SSM_CHUNK
    pad = LANES - SSM_HEADS
    dtb = jnp.pad(dt_bias, (0, pad)).reshape(1, LANES)
    alog = jnp.pad(a_log, (0, pad)).reshape(1, LANES)
    d_x = jnp.repeat(d_skip, SSM_HEADDIM).reshape(1, SSM_D_INNER)
    row = lambda n: pl.BlockSpec((1, n), lambda i, j: (0, 0))
    return pl.pallas_call(
        _ssd_kernel,
        grid=(b, s // L),
        in_specs=[pl.BlockSpec((1, L, SSM_XBC), lambda i, j: (i, j, 0)),
                  pl.BlockSpec((1, L, SSM_D_INNER), lambda i, j: (i, j, 0)),
                  pl.BlockSpec((1, L, LANES), lambda i, j: (i, j, 0)),
                  pl.BlockSpec((SSM_CONV, SSM_XBC), lambda i, j: (0, 0)),
                  row(SSM_XBC), row(LANES), row(LANES), row(SSM_D_INNER), row(SSM_D_INNER)],
        out_specs=pl.BlockSpec((1, L, SSM_D_INNER), lambda i, j: (i, j, 0)),
        out_shape=jax.ShapeDtypeStruct((b, s, SSM_D_INNER), BF16),
        scratch_shapes=[pltpu.VMEM((SUBLANES + L, SSM_XBC), F32),
                        pltpu.VMEM((SSM_GROUPS, SSM_STATE, GROUP_W), F32),
                        pltpu.VMEM((L, SSM_D_INNER), F32)],
        compiler_params=_cparams(("arbitrary", "arbitrary")),
        name="ssd_mixer",
    )(xbc, z, dt_raw, conv_w, conv_b.reshape(1, SSM_XBC), dtb, alog, d_x,
      norm_g.reshape(1, SSM_D_INNER))


def _dil_kernel(q_ref, k_ref, v_ref, bias_ref, o_ref, lse_ref, *, nb):
    blk = DIL_BLOCK
    sub = lax.broadcasted_iota(jnp.int32, (LANES, 1), 0)
    lane = lax.broadcasted_iota(jnp.int32, (1, LANES), 1)

    def vt_aug(v):
        vt = v.astype(F32).T
        out = []
        for hh in range(PAIR):
            in_head = (sub >= hh * HEAD_DIM) & (sub < (hh + 1) * HEAD_DIM)
            ones_row = sub == (1 - hh) * HEAD_DIM
            out.append(jnp.where(in_head, vt, jnp.where(ones_row, 1.0, 0.0)).astype(BF16))
        return out

    def trip(t, carry):
        k_prev, vt_prev = carry[0], (carry[1], carry[2])
        work = []
        for u in range(DIL_UNROLL):
            n = t * DIL_UNROLL + u
            r0 = pl.multiple_of(n * blk, blk)
            q = q_ref[0, pl.ds(r0, blk), :]
            k_cur = k_ref[0, pl.ds(r0, blk), :]
            for hh in range(PAIR):
                in_head = (lane >= hh * HEAD_DIM) & (lane < (hh + 1) * HEAD_DIM)
                qh = jnp.where(in_head, q, jnp.zeros_like(q)) * (HEAD_DIM ** -0.5)
                work.append((n, r0, hh, _dot_nt(k_prev, qh), _dot_nt(k_cur, qh)))
            k_prev = k_cur
        soft = []
        for n, r0, hh, s_prev, s_cur in work:
            s_prev = jnp.where(n > 0, s_prev + bias_ref[hh, 0, 0:blk, :], NEG)
            s_cur = s_cur + bias_ref[hh, 0, blk:2 * blk, :]
            m = jnp.maximum(jnp.max(s_prev, axis=0, keepdims=True), jnp.max(s_cur, axis=0, keepdims=True))
            soft.append((m, jnp.exp(s_prev - m).astype(BF16), jnp.exp(s_cur - m).astype(BF16)))
        res = []
        for u in range(DIL_UNROLL):
            n, r0 = work[u * PAIR][0], work[u * PAIR][1]
            vt_cur = vt_aug(v_ref[0, pl.ds(r0, blk), :])
            for hh in range(PAIR):
                m, e_prev, e_cur = soft[u * PAIR + hh]
                res.append((m, _dot(vt_prev[hh], e_prev) + _dot(vt_cur[hh], e_cur)))
            vt_prev = vt_cur
        for u in range(DIL_UNROLL):
            r0 = work[u * PAIR][1]
            outs, lses = [], []
            for hh in range(PAIR):
                m, acc = res[u * PAIR + hh]
                den = acc[(1 - hh) * HEAD_DIM:(1 - hh) * HEAD_DIM + 1, :]
                outs.append(acc / den)
                lses.append(m + jnp.log(den))
            o_ref[0, pl.ds(r0, blk), :] = jnp.where(sub < HEAD_DIM, outs[0], outs[1]).T
            lse_ref[0, pl.ds(r0, blk), :] = jnp.where(sub < HEAD_DIM, lses[0], lses[1]).T
        return k_prev, vt_prev[0], vt_prev[1]

    vt_first = vt_aug(v_ref[0, 0:blk, :])
    init = (k_ref[0, 0:blk, :], vt_first[0], vt_first[1])
    lax.fori_loop(0, nb // DIL_UNROLL, trip, init)


def _dil_branch(q, k, v, bias, branch, dil):
    b, s, w = q.shape
    L = s // dil
    nb = L // DIL_BLOCK
    assert L % DIL_BLOCK == 0 and nb % DIL_UNROLL == 0, (s, dil)
    npair = w // LANES
    view =lambda a: a.reshape(b, L, dil * w)
    spec = pl.BlockSpec((1, L, LANES), lambda i, r, p: (i, 0, r * npair + p))
    o, lse = pl.pallas_call(
        functools.partial(_dil_kernel, nb=nb),
        grid=(b, dil, npair),
        in_specs=[spec, spec, spec,
                  pl.BlockSpec((PAIR, 1, 2 * DIL_BLOCK, DIL_BLOCK), lambda i, r, p: (p, branch, 0, 0))],
        out_specs=[spec, spec],
        out_shape=[jax.ShapeDtypeStruct((b, L, dil * w), F32)] * 2,
        compiler_params=_cparams(("arbitrary", "arbitrary", "arbitrary")),
        name=f"dilated_branch{branch}",
    )(view(q), view(k), view(v), bias)
    return o.reshape(b * s, w), lse.reshape(b * s, w)


def _dil_merge_kernel(*refs):
    n = (len(refs) - 1) // 2
    o_refs, l_refs, out_ref = refs[:n], refs[n:2 * n], refs[2 * n]
    lses = [r[...] for r in l_refs]
    m = functools.reduce(jnp.maximum, lses)
    ws = [jnp.exp(l - m) for l in lses]
    num = sum(w * r[...] for w, r in zip(ws, o_refs))
    out_ref[...] = (num / sum(ws)).astype(out_ref.dtype)


def _dil_merge(outs, lses, tm):
    t, w = outs[0].shape
    spec = pl.BlockSpec((tm, w), lambda i: (i, 0))
    return pl.pallas_call(
        _dil_merge_kernel,
        grid=(t // tm,),
        in_specs=[spec] * (2 * len(outs)),
        out_specs=spec,
        out_shape=jax.ShapeDtypeStruct((t, w), BF16),
        compiler_params=_cparams(("parallel",)),
        name="dilated_merge",
    )(*outs, *lses)


def _outproj_ln_kernel(*refs, n_in, alpha):
    a_refs, w_refs = refs[:n_in], refs[n_in:2 * n_in]
    x_ref, g_ref, b_ref, o_ref = refs[2 * n_in:]
    acc = _dot(a_refs[0][...], w_refs[0][...])
    for a_ref, w_ref in zip(a_refs[1:], w_refs[1:]):
        acc = acc + _dot(a_ref[...], w_ref[...])
    o_ref[...] = _layer_norm(alpha * x_ref[...] + acc, g_ref[...], b_ref[...])


def _outproj_ln(acts, weights, x2, g, b, alpha, tm):
    t, d = x2.shape
    n_in = len(acts)
    in_specs = ([pl.BlockSpec((tm, a.shape[1]), lambda i: (i, 0)) for a in acts]
                + [pl.BlockSpec(w.shape, lambda i: (0, 0)) for w in weights]
                + [pl.BlockSpec((tm, d), lambda i: (i, 0)),
                   pl.BlockSpec((1, d), lambda i: (0, 0)), pl.BlockSpec((1, d), lambda i: (0, 0))])
    return pl.pallas_call(
        functools.partial(_outproj_ln_kernel, n_in=n_in, alpha=alpha),
        grid=(t // tm,),
        in_specs=in_specs,
        out_specs=pl.BlockSpec((tm, d), lambda i: (i, 0)),
        out_shape=jax.ShapeDtypeStruct((t, d), F32),
        compiler_params=_cparams(("parallel",)),
        name="out_proj_ln",
    )(*acts, *weights, x2, g.reshape(1, d), b.reshape(1, d))


def _ffn_kernel(x_ref, wg_ref, wu_ref, wd_ref, g_ref, b_ref, o_ref, acc_ref, *, alpha):
    f = pl.program_id(1)

    @pl.when(f == 0)
    def _():
        acc_ref[...] = jnp.zeros(acc_ref.shape, F32)

    xb = x_ref[...].astype(BF16)
    gate = _dot(xb, wg_ref[...])
    up = _dot(xb, wu_ref[...])
    acc_ref[...] += _dot((_silu(gate) * up).astype(BF16), wd_ref[...])

    @pl.when(f == pl.num_programs(1) - 1)
    def _():
        o_ref[...] = _layer_norm(alpha * x_ref[...] + acc_ref[...], g_ref[...], b_ref[...])


def _ffn_ln(x2, w_in, w_down, g, b, alpha, tm, tf):
    t, d = x2.shape
    ff = w_down.shape[0]
    nf = ff // tf
    return pl.pallas_call(
        functools.partial(_ffn_kernel, alpha=alpha),
        grid=(t // tm, nf),
        in_specs=[pl.BlockSpec((tm, d), lambda i, f: (i, 0)),
                  pl.BlockSpec((d, tf), lambda i, f: (0, f)),
                  pl.BlockSpec((d, tf), lambda i, f: (0, f + nf)),
                  pl.BlockSpec((tf, d), lambda i, f: (f, 0)),
                  pl.BlockSpec((1, d), lambda i, f: (0, 0)),
                  pl.BlockSpec((1, d), lambda i, f: (0, 0))],
        out_specs=pl.BlockSpec((tm, d), lambda i, f: (i, 0)),
        out_shape=jax.ShapeDtypeStruct((t, d), F32),
        scratch_shapes=[pltpu.VMEM((tm, d), F32)],
        compiler_params=_cparams(("parallel", "arbitrary")),
        name="ffn_ln",
    )(x2, w_in, w_in, w_down, g.reshape(1, d), b.reshape(1, d))


MOE_TM = 1024
MOE_SUB = 128


def _router_kernel(x_ref, wr_ref, c_ref, pos_ref, post_ref, cnt_ref):
    tm = x_ref.shape[0]
    logits = _dot(x_ref[...], wr_ref[...], precision=HIGHEST)
    lanef = lax.broadcasted_iota(jnp.int32, logits.shape, 1).astype(F32)
    cur = jnp.where(lanef < N_EXPERTS, logits, -jnp.inf)
    vals, hits = [], []
    for _ in range(MOE_TOPK):
        best = jnp.max(cur, axis=1, keepdims=True)
        first = jnp.min(jnp.where(cur == best, lanef, float(LANES)), axis=1, keepdims=True)
        hit = lanef == first
        vals.append(best)
        hits.append(hit)
        cur = jnp.where(hit, -jnp.inf, cur)
    es = [jnp.exp(v - vals[0]) for v in vals]
    den = sum(es)
    comb = jnp.zeros(logits.shape, F32)
    routed = jnp.zeros(logits.shape, F32)
    for e, hit in zip(es, hits):
        comb = jnp.where(hit, e / den, comb)
        routed = jnp.where(hit, 1.0, routed)
    c_ref[...] = comb
    r = lax.broadcasted_iota(jnp.int32, (tm, tm), 0)
    c = lax.broadcasted_iota(jnp.int32, (tm, tm), 1)
    incl = _dot(jnp.where(c <= r, 1.0, 0.0).astype(BF16), routed.astype(BF16))
    pos = jnp.where(routed > 0.5, incl - 1.0, -1.0)
    pos_ref[...] = pos
    post_ref[0] = pos.T[:SUBLANES, :]
    cnt_ref[...] = jnp.broadcast_to(incl[tm - 1:tm, :], cnt_ref.shape)


def _router(x2, w_router, tm):
    t, d = x2.shape
    nt = t // tm
    wr = jnp.pad(w_router, ((0, 0), (0, LANES - N_EXPERTS)))
    return pl.pallas_call(
        _router_kernel,
        grid=(nt,),
        in_specs=[pl.BlockSpec((tm, d), lambda i: (i, 0)),
                  pl.BlockSpec((d, LANES), lambda i: (0, 0))],
        out_specs=[pl.BlockSpec((tm, LANES), lambda i: (i, 0)),
                   pl.BlockSpec((tm, LANES), lambda i: (i, 0)),
                   pl.BlockSpec((1, SUBLANES, tm), lambda i: (i, 0, 0)),
                   pl.BlockSpec((SUBLANES, LANES), lambda i: (i, 0))],
        out_shape=[jax.ShapeDtypeStruct((t, LANES), F32),
                   jax.ShapeDtypeStruct((t, LANES), F32),
                   jax.ShapeDtypeStruct((nt, SUBLANES, tm), F32),
                   jax.ShapeDtypeStruct((nt * SUBLANES, LANES), F32)],
        compiler_params=_cparams(("parallel",)),
        name="moe_router",
    )(x2, wr)


def _moe_kernel(nsub_ref, x_ref, c_ref, pos_ref, post_ref, wg_ref, wu_ref, wd_ref, g_ref, b_ref, o_ref,
                xb_ref, xs_ref, gs_ref, oe_ref, *, alpha):
    i = pl.program_id(0)
    e = pl.program_id(1)
    f = pl.program_id(2)
    last_f = pl.num_programs(2) - 1
    tm = x_ref.shape[0]
    sub = MOE_SUB
    n = nsub_ref[i * N_EXPERTS + e]
    lane = lax.broadcasted_iota(jnp.int32, (1, LANES), 1)

    @pl.when((e == 0) & (f == 0))
    def _():
        o_ref[...] = jnp.zeros(o_ref.shape, F32)
        xb_ref[...] = x_ref[...].astype(BF16)

    @pl.when(f == 0)
    def _():
        comb = c_ref[...]
        c_hi = comb.astype(BF16)
        c_lo = (comb - c_hi.astype(F32)).astype(BF16)
        prow = post_ref[0, pl.ds(e, 1), :]

        def gather(s, carry):
            r0 = pl.multiple_of(s * sub, sub)
            want = (lax.broadcasted_iota(jnp.int32, (sub, tm), 0) + s * sub).astype(F32)
            sel = jnp.where(prow == want, 1.0, 0.0).astype(BF16)
            xs_ref[pl.ds(r0, sub), :] = _dot(sel, xb_ref[...]).astype(BF16)
            cg = _dot(sel, c_hi) + _dot(sel, c_lo)
            ce = jnp.sum(jnp.where(lane == e, cg, 0.0), axis=1, keepdims=True)
            gs_ref[pl.ds(r0, sub), :] = jnp.broadcast_to(ce, (sub, LANES))
            return carry

        lax.fori_loop(0, n, gather, 0)

    def ffn(s, carry):
        r0 = pl.multiple_of(s * sub, sub)
        xs = xs_ref[pl.ds(r0, sub), :]
        gate = _dot(xs, wg_ref[0])
        up = _dot(xs, wu_ref[0])
        act = (_silu(gate) * up * gs_ref[pl.ds(r0, sub), 0:1]).astype(BF16)
        contrib = _dot(act, wd_ref[0])

        @pl.when(f == 0)
        def _():
            oe_ref[pl.ds(r0, sub), :] = contrib

        @pl.when(f > 0)
        def _():
            oe_ref[pl.ds(r0, sub), :] += contrib

        return carry

    lax.fori_loop(0, n, ffn, 0)

    @pl.when(f == last_f)
    def _():
        pos = pos_ref[...]
        pcol = jnp.sum(jnp.where(lane == e, pos, 0.0), axis=1, keepdims=True)

        def scatter(s, carry):
            r0 = pl.multiple_of(s * sub, sub)
            want = (lax.broadcasted_iota(jnp.int32, (tm, sub), 1) + s * sub).astype(F32)
            sel_t = jnp.where(pcol == want, 1.0, 0.0).astype(BF16)
            o_ref[...] += _dot(sel_t, oe_ref[pl.ds(r0, sub), :].astype(BF16))
            return carry

        lax.fori_loop(0, n, scatter, 0)

    @pl.when((e == pl.num_programs(1) - 1) & (f == last_f))
    def _():
        o_ref[...] = _layer_norm(alpha * x_ref[...] + o_ref[...], g_ref[...], b_ref[...])


def _moe_ln(x2, routing, w_in, w_down, g, b, alpha, tm, tf):
    comb, pos, pos_t, cnt = routing
    t, d = x2.shape
    ne, ff, _ = w_down.shape
    nf = ff // tf
    nt = t // tm
    counts = cnt.reshape(nt, SUBLANES, LANES)[:, 0, :ne]
    nsub = jnp.ceil(counts / MOE_SUB).astype(jnp.int32).reshape(nt * ne)
    grid_spec = pltpu.PrefetchScalarGridSpec(
        num_scalar_prefetch=1,
        grid=(nt, ne, nf),
        in_specs=[pl.BlockSpec((tm, d), lambda i, e, f, ns: (i, 0)),
                  pl.BlockSpec((tm, LANES), lambda i, e, f, ns: (i, 0)),
                  pl.BlockSpec((tm, LANES), lambda i, e, f, ns: (i, 0)),
                  pl.BlockSpec((1, SUBLANES, tm), lambda i, e, f, ns: (i, 0, 0)),
                  pl.BlockSpec((1, d, tf), lambda i, e, f, ns: (e, 0, f)),
                  pl.BlockSpec((1, d, tf), lambda i, e, f, ns: (e, 0, f + nf)),
                  pl.BlockSpec((1, tf, d), lambda i, e, f, ns: (e, f, 0)),
                  pl.BlockSpec((1, d), lambda i, e, f, ns: (0, 0)),
                  pl.BlockSpec((1, d), lambda i, e, f, ns: (0, 0))],
        out_specs=pl.BlockSpec((tm, d), lambda i, e, f, ns: (i, 0)),
        scratch_shapes=[pltpu.VMEM((tm, d), BF16),
                        pltpu.VMEM((tm, d), BF16),
                        pltpu.VMEM((tm, LANES), F32),
                        pltpu.VMEM((tm, d), F32)])
    return pl.pallas_call(
        functools.partial(_moe_kernel, alpha=alpha),
        grid_spec=grid_spec,
        out_shape=jax.ShapeDtypeStruct((t, d), F32),
        compiler_params=pltpu.CompilerParams(dimension_semantics=("parallel", "arbitrary", "arbitrary"),
                                             vmem_limit_bytes=MOE_VMEM_LIMIT_BYTES),
        name="moe_ln",
    )(nsub, x2, comb, pos, pos_t, w_in, w_in, w_down, g.reshape(1, d), b.reshape(1, d))


def _tile(t, want):
    while t % want:
        want //= 2
    return want


def _even_layer(x2, bsz, seq, w_in, conv_w, conv_b, dt_bias, a_log, d_skip, norm_g, w_out,
                moba_bias, ln_g, ln_b, alpha):
    t = x2.shape[0]
    w_cat = jnp.concatenate(
        [w_in[:, :EVEN_MAIN], jnp.pad(w_in[:, EVEN_MAIN:], ((0, 0), (0, LANES - SSM_HEADS)))], axis=1
    ).astype(BF16)
    bounds = (0, MOBA_W, 2 * MOBA_W, 3 * MOBA_W, 3 * MOBA_W + SSM_D_INNER, EVEN_MAIN, EVEN_MAIN + LANES)
    segs = tuple(zip(bounds[:-1], bounds[1:]))
    q, k, v, z, xbc, dt_raw = _project(x2, w_cat, segs, (BF16, BF16, BF16, F32, F32, F32), _tile(t, 256))
    r3 = lambda a: a.reshape(bsz, seq, a.shape[-1])
    attn = _moba_attention(r3(q), r3(k), r3(v), moba_bias).reshape(t, MOBA_W)
    ssm = _ssd_mixer(r3(xbc), r3(z), r3(dt_raw), conv_w, conv_b, dt_bias, a_log, d_skip,
                     norm_g).reshape(t, SSM_D_INNER)
    w_o = w_out.astype(BF16)
    return _outproj_ln([attn, ssm], [w_o[:MOBA_W], w_o[MOBA_W:]], x2, ln_g, ln_b, alpha, _tile(t, 512))


def _odd_layer(x2, bsz, seq, w_qkv, w_out, dil_bias, ln_g, ln_b, alpha):
    t = x2.shape[0]
    w = DIL_HEADS * HEAD_DIM
    segs = ((0, w), (w, 2 * w), (2 * w, 3 * w))
    q, k, v = _project(x2, w_qkv.astype(BF16), segs, (BF16, BF16, BF16), _tile(t, 256))
    r3 = lambda a: a.reshape(bsz, seq, w)
    outs, lses = [], []
    for branch, (_, dil) in enumerate(DIL_BRANCHES):
        o, lse = _dil_branch(r3(q), r3(k), r3(v), dil_bias, branch, dil)
        outs.append(o)
        lses.append(lse)
    merged = _dil_merge(outs, lses, _tile(t, 512))
    return _outproj_ln([merged], [w_out.astype(BF16)], x2, ln_g, ln_b, alpha, _tile(t, 512))


def kernel(x, rel_bias, ln_mix_g, ln_mix_b, ln_ffn_g, ln_ffn_b, even_w_in, even_conv_w, even_conv_b,
           even_dt_bias, even_A_log, even_D, even_norm_g, even_w_out, ffn_w_in, ffn_w_down,
           odd_w_qkv, odd_w_out, moe_router, moe_w_in, moe_w_down):
    bsz, seq, d = x.shape
    depth = ln_mix_g.shape[0]
    alpha = (2.0 * depth) ** 0.25
    t = bsz * seq
    x2 = x.reshape(t, d)
    moba_bias = _bias_tiles(rel_bias, _moba_bucket_idx(seq // MOBA_BLOCK), MOBA_HEADS)
    dil_bias = _bias_tiles(rel_bias, _dil_bucket_idx(), DIL_HEADS)
    for l in range(depth):
        e = l // 2
        if l % 2 == 0:
            x2 = _even_layer(x2, bsz, seq, even_w_in[e], even_conv_w[e], even_conv_b[e], even_dt_bias[e],
                             even_A_log[e], even_D[e], even_norm_g[e], even_w_out[e], moba_bias,
                             ln_mix_g[l], ln_mix_b[l], alpha)
            x2 = _ffn_ln(x2, ffn_w_in[e].astype(BF16), ffn_w_down[e].astype(BF16), ln_ffn_g[l], ln_ffn_b[l],
                         alpha, _tile(t, 512), FFN_DENSE // 2)
        else:
            x2 = _odd_layer(x2, bsz, seq, odd_w_qkv[e], odd_w_out[e], dil_bias, ln_mix_g[l], ln_mix_b[l], alpha)
            tm = _tile(t, MOE_TM)
            routing = _router(x2, moe_router[e], tm)
            x2 = _moe_ln(x2, routing, moe_w_in[e].astype(BF16), moe_w_down[e].astype(BF16), ln_ffn_g[l],
                         ln_ffn_b[l], alpha, tm, FFN_EXPERT // 4)
    return x2.reshape(bsz, seq, d)
```

```python
import functools
import math

import jax
import jax.numpy as jnp
from jax import lax
from jax.experimental import pallas as pl
from jax.experimental.pallas import tpu as pltpu

F32 = jnp.float32
BF16 = jnp.bfloat16
HIGHEST = lax.Precision.HIGHEST

LANES = 128
SUBLANES = 8
VMEM_LIMIT_BYTES = 48 * 1024 * 1024
MOE_VMEM_LIMIT_BYTES = 54 * 1024 * 1024

D_MODEL = 1024
HEAD_DIM = 64
MOBA_HEADS = 8
MOBA_BLOCK = 256
MOBA_TOPK = 3
MOBA_UNROLL = 2
SSM_D_INNER = D_MODEL
SSM_HEADDIM = 64
SSM_HEADS = SSM_D_INNER // SSM_HEADDIM
SSM_GROUPS = 4
SSM_STATE = 128
SSM_CONV = 4
SSM_CHUNK = 128
DIL_HEADS = D_MODEL // HEAD_DIM
DIL_BRANCHES = ((128, 1), (512, 4), (2048, 16))
DIL_BLOCK = 128
DIL_UNROLL = 2
REL_BUCKETS = 32
REL_MAX_DIST = 2048
REL_HEADS = max(MOBA_HEADS, DIL_HEADS)
FFN_DENSE = 2816
N_EXPERTS = 8
MOE_TOPK = 2
FFN_EXPERT = 3584
LN_EPS = 1e-5
RMS_EPS = 1e-5
MOBA_W = MOBA_HEADS * HEAD_DIM
SSM_BC = SSM_GROUPS * SSM_STATE
SSM_XBC = SSM_D_INNER + 2 * SSM_BC
EVEN_MAIN = 3 * MOBA_W + SSM_D_INNER + SSM_XBC
GROUP_W = SSM_D_INNER // SSM_GROUPS

NEG = -1e30
PAIR = LANES // HEAD_DIM


def _cparams(sem):
    return pltpu.CompilerParams(dimension_semantics=sem, vmem_limit_bytes=VMEM_LIMIT_BYTES)


def _dot(a, b, **kw):
    return jnp.dot(a, b, preferred_element_type=F32, **kw)


def _dot_nt(a, b):
    return lax.dot_general(a, b, (((1,), (1,)), ((), ())), preferred_element_type=F32)


def _silu(x):
    return x * jax.nn.sigmoid(x)


def _layer_norm(y, g, b):
    mu = jnp.mean(y, axis=-1, keepdims=True)
    yc = y - mu
    var = jnp.mean(yc * yc, axis=-1, keepdims=True)
    return yc * lax.rsqrt(var + LN_EPS) * g + b


def _bias_tile_kernel(rb_ref, idx_ref, o_ref):
    h = pl.program_id(0)
    idx = idx_ref[0]
    acc = jnp.zeros(idx.shape, F32)
    for b in range(REL_BUCKETS):
        acc = jnp.where(idx == b, rb_ref[b, h], acc)
    o_ref[0, 0] = jnp.where(idx < 0, NEG, acc)


def _bias_tiles(rel_bias, idx, heads):
    n, r, c = idx.shape
    return pl.pallas_call(
        _bias_tile_kernel,
        grid=(heads, n),
        in_specs=[pl.BlockSpec(memory_space=pltpu.SMEM),
                  pl.BlockSpec((1, r, c), lambda h, i: (i, 0, 0))],
        out_specs=pl.BlockSpec((1, 1, r, c), lambda h, i: (h, i, 0, 0)),
        out_shape=jax.ShapeDtypeStruct((heads, n, r, c), F32),
        compiler_params=_cparams(("arbitrary", "arbitrary")),
        name="bias_tiles",
    )(rel_bias, idx)


def _rel_bucket(dist):
    max_exact = REL_BUCKETS // 2
    d = jnp.maximum(dist, 0)
    df = jnp.maximum(d, 1).astype(F32)
    large = max_exact + (jnp.log(df / max_exact) / math.log(REL_MAX_DIST / max_exact)
                         * (REL_BUCKETS - max_exact)).astype(jnp.int32)
    large = jnp.minimum(large, REL_BUCKETS - 1)
    return jnp.where(d < max_exact, d, large)


def _moba_bucket_idx(nblk):
    j = jnp.arange(MOBA_BLOCK, dtype=jnp.int32)[:, None]
    i = jnp.arange(MOBA_BLOCK, dtype=jnp.int32)[None, :]
    delta = jnp.arange(nblk, dtype=jnp.int32)[:, None, None]
    dist = delta * MOBA_BLOCK + i - j
    return jnp.where(dist >= 0, _rel_bucket(dist), -1)


def _dil_bucket_idx():
    ki = jnp.arange(2 * DIL_BLOCK, dtype=jnp.int32)[:, None]
    qi = jnp.arange(DIL_BLOCK, dtype=jnp.int32)[None, :]
    steps = qi + DIL_BLOCK - ki
    tiles = []
    for window, dil in DIL_BRANCHES:
        span = window // dil
        valid = (steps >= 0) & (steps <= span)
        tiles.append(jnp.where(valid, _rel_bucket(steps * dil), -1))
    return jnp.stack(tiles)


def _proj_kernel(x_ref, w_ref, *o_refs, segs):
    xb = x_ref[...].astype(BF16)
    for o_ref, (c0, c1) in zip(o_refs, segs):
        o_ref[...] = _dot(xb, w_ref[:, c0:c1]).astype(o_ref.dtype)


def _project(x2, w_bf16, segs, dtypes, tm):
    t, d = x2.shape
    n = w_bf16.shape[1]
    return pl.pallas_call(
        functools.partial(_proj_kernel, segs=segs),
        grid=(t // tm,),
        in_specs=[pl.BlockSpec((tm, d), lambda i: (i, 0)),
                  pl.BlockSpec((d, n), lambda i: (0, 0))],
        out_specs=[pl.BlockSpec((tm, c1 - c0), lambda i: (i, 0)) for c0, c1 in segs],
        out_shape=[jax.ShapeDtypeStruct((t, c1 - c0), dt) for (c0, c1), dt in zip(segs, dtypes)],
        compiler_params=_cparams(("parallel",)),
        name="in_proj",
    )(x2, w_bf16)


def _moba_kernel(q_ref, k_ref, v_ref, bias_ref, o_ref, kmean_ref, vt_ref, mask_ref, *, nblk):
    qi = pl.program_id(2)
    blk = MOBA_BLOCK
    seq = k_ref.shape[1]
    lane = lax.broadcasted_iota(jnp.int32, (1, LANES), 1)
    sub = lax.broadcasted_iota(jnp.int32, (LANES, 1), 0)

    @pl.when(qi == 0)
    def _():
        r = lax.broadcasted_iota(jnp.int32, (nblk, seq), 0)
        c = lax.broadcasted_iota(jnp.int32, (nblk, seq), 1)
        avg = jnp.where((c >= r * blk) & (c < (r + 1) * blk), 1.0 / blk, 0.0).astype(BF16)
        kmean_ref[...] = _dot(avg, k_ref[0])

        def build(j, carry):
            r0 = pl.multiple_of(j * blk, blk)
            vt = v_ref[0, pl.ds(r0, blk), :].astype(F32).T
            for hh in range(PAIR):
                in_head = (sub >= hh * HEAD_DIM) & (sub < (hh + 1) * HEAD_DIM)
                ones_row = sub == (1 - hh) * HEAD_DIM
                vt_ref[hh, j] = jnp.where(in_head, vt, jnp.where(ones_row, 1.0, 0.0)).astype(BF16)
            return carry

        lax.fori_loop(0, nblk, build, 0)

    q = q_ref[0]
    kmean = kmean_ref[...]
    bf = lax.broadcasted_iota(jnp.int32, (nblk, blk), 0).astype(F32)
    qif = qi.astype(F32)

    qhs = []
    for hh in range(PAIR):
        in_head = (lane >= hh * HEAD_DIM) & (lane < (hh + 1) * HEAD_DIM)
        km = jnp.where(in_head, kmean, 0.0)
        km_hi = km.astype(BF16)
        km_lo = (km - km_hi.astype(F32)).astype(BF16)
        gate = _dot_nt(km_hi, q) + _dot_nt(km_lo, q)
        gate = jnp.where(bf < qif, gate, -jnp.inf)
        sel = jnp.where(bf == qif, 1.0, 0.0)
        for _ in range(MOBA_TOPK):
            best = jnp.max(gate, axis=0, keepdims=True)
            first = jnp.min(jnp.where(gate == best, bf, float(nblk)), axis=0, keepdims=True)
            hit = bf == first
            sel = jnp.where(hit & (best > -jnp.inf), 1.0, sel)
            gate = jnp.where(hit, -jnp.inf, gate)
        mask_ref[hh] = jnp.where(sel > 0.5, 0.0, NEG)
        qhs.append(jnp.where(in_head, q, jnp.zeros_like(q)) * (HEAD_DIM ** -0.5))

    def body(t, carry):
        j0 = t * MOBA_UNROLL
        r0 = pl.multiple_of(j0 * blk, MOBA_UNROLL * blk)
        k_grp = k_ref[0, pl.ds(r0, MOBA_UNROLL * blk), :]
        s_alls = [_dot_nt(k_grp, qhs[hh]) for hh in range(PAIR)]
        soft = []
        for hh in range(PAIR):
            m = carry[2 * hh]
            parts = []
            for u in range(MOBA_UNROLL):
                j = j0 + u
                delta = jnp.maximum(qi - j, 0)
                parts.append(s_alls[hh][u * blk:(u + 1) * blk] + bias_ref[hh, delta]
                             + mask_ref[hh, pl.ds(j, 1), :])
            m_new = functools.reduce(jnp.maximum, [m] + [jnp.max(s, axis=0, keepdims=True) for s in parts])
            soft.append((m_new, jnp.exp(m - m_new), [jnp.exp(s - m_new).astype(BF16) for s in parts]))
        new = []
        for hh in range(PAIR):
            m_new, alpha, ps = soft[hh]
            acc = alpha * carry[2 * hh + 1]
            for u, p in enumerate(ps):
                acc = acc + _dot(vt_ref[hh, j0 + u], p)
            new += [m_new, acc]
        return tuple(new)

    m_init = jnp.full((1, blk), 4 * NEG, F32)
    acc_init = jnp.zeros((LANES, blk), F32)
    trips = (qi + MOBA_UNROLL) // MOBA_UNROLL
    fin = lax.fori_loop(0, trips, body, (m_init, acc_init, m_init, acc_init))
    outs = []
    for hh in range(PAIR):
        acc = fin[2 * hh + 1]
        den = acc[(1 - hh) * HEAD_DIM:(1 - hh) * HEAD_DIM + 1, :]
        outs.append(acc / den)
    o_t = jnp.where(sub < HEAD_DIM, outs[0], outs[1])
    o_ref[0] = o_t.T.astype(o_ref.dtype)


def _moba_attention(q, k, v, bias):
    b, s, w = q.shape
    nblk = s // MOBA_BLOCK
    assert s % MOBA_BLOCK == 0 and nblk % MOBA_UNROLL == 0, s
    npair = w // LANES
    return pl.pallas_call(
        functools.partial(_moba_kernel, nblk=nblk),
        grid=(npair, b, nblk),
        in_specs=[pl.BlockSpec((1, MOBA_BLOCK, LANES), lambda p, i, j: (i, j, p)),
                  pl.BlockSpec((1, s, LANES), lambda p, i, j: (i, 0, p)),
                  pl.BlockSpec((1, s, LANES), lambda p, i, j: (i, 0, p)),
                  pl.BlockSpec((PAIR, nblk, MOBA_BLOCK, MOBA_BLOCK), lambda p, i, j: (p, 0, 0, 0))],
        out_specs=pl.BlockSpec((1, MOBA_BLOCK, LANES), lambda p, i, j: (i, j, p)),
        out_shape=jax.ShapeDtypeStruct((b, s, w), BF16),
        scratch_shapes=[pltpu.VMEM((nblk, LANES), F32),
                        pltpu.VMEM((PAIR, nblk, LANES, MOBA_BLOCK), BF16),
                        pltpu.VMEM((PAIR, nblk, MOBA_BLOCK), F32)],
        compiler_params=_cparams(("arbitrary", "arbitrary", "arbitrary")),
        name="moba_attention",
    )(q, k, v, bias)


def _ssd_kernel(xbc_ref, z_ref, dt_ref, cw_ref, cb_ref, dtb_ref, alog_ref, dx_ref, ng_ref,
                o_ref, ext_ref, h_ref, y_ref):
    c = pl.program_id(1)
    L = SSM_CHUNK
    tail = SUBLANES

    @pl.when(c == 0)
    def _():
        ext_ref[0:tail, :] = jnp.zeros((tail, SSM_XBC), F32)
        h_ref[...] = jnp.zeros(h_ref.shape, F32)

    u = xbc_ref[0]
    ext_ref[tail:tail + L, :] = u
    conv = u * cw_ref[SSM_CONV - 1:SSM_CONV, :] + cb_ref[...]
    for j in range(1, SSM_CONV):
        conv = conv + ext_ref[tail - j:tail - j + L, :] * cw_ref[SSM_CONV - 1 - j:SSM_CONV - j, :]
    ext_ref[0:tail, :] = u[L - tail:L, :]
    xc = _silu(conv)
    xs = xc[:, :SSM_D_INNER]

    x_dt = dt_ref[0] + dtb_ref[...]
    dt = jnp.maximum(x_dt, 0.0) + jnp.log1p(jnp.exp(-jnp.abs(x_dt)))
    a = dt * (-jnp.exp(alog_ref[...]))
    row = lax.broadcasted_iota(jnp.int32, (L, L), 0)
    col = lax.broadcasted_iota(jnp.int32, (L, L), 1)
    causal = row >= col
    a_cs = _dot(jnp.where(causal, 1.0, 0.0), a, precision=HIGHEST)
    a_cs_t = a_cs.T
    er = lax.broadcasted_iota(jnp.int32, (LANES, SSM_D_INNER), 0)
    ec = lax.broadcasted_iota(jnp.int32, (LANES, SSM_D_INNER), 1)
    expand = jnp.where((ec >= er * SSM_HEADDIM) & (ec < (er + 1) * SSM_HEADDIM), 1.0, 0.0)
    a_cs_x = _dot(a_cs, expand, precision=HIGHEST)
    dt_x = _dot(dt, expand, precision=HIGHEST)
    a_last_x = a_cs_x[L - 1:L, :]
    xdt = xs * dt_x
    w_end = xdt * jnp.exp(a_last_x - a_cs_x)
    decay_in = jnp.exp(a_cs_x)
    chunk_decay = jnp.exp(a_last_x)
    lane = lax.broadcasted_iota(jnp.int32, (1, LANES), 1)
    hpg = SSM_HEADS // SSM_GROUPS

    for g in range(SSM_GROUPS):
        gs = slice(g * GROUP_W, (g + 1) * GROUP_W)
        b_g = xc[:, SSM_D_INNER + g * SSM_STATE:SSM_D_INNER + (g + 1) * SSM_STATE]
        c_g = xc[:, SSM_D_INNER + SSM_BC + g * SSM_STATE:SSM_D_INNER + SSM_BC + (g + 1) * SSM_STATE]
        b_bf = b_g.astype(BF16)
        c_bf = c_g.astype(BF16)
        cb = _dot_nt(c_bf, b_bf)
        h_in = h_ref[g]
        y_off = _dot(c_bf, h_in.astype(BF16)) * decay_in[:, gs]
        st = _dot(b_g.T.astype(BF16), w_end[:, gs].astype(BF16))
        h_ref[g] = h_in * chunk_decay[:, gs] + st
        for pr in range(hpg // PAIR):
            ps = slice(g * GROUP_W + pr * LANES, g * GROUP_W + (pr + 1) * LANES)
            x_pair = xdt[:, ps].astype(BF16)
            ys = []
            for hh in range(PAIR):
                h = g * hpg + pr * PAIR + hh
                seg = a_cs[:, h:h + 1] - a_cs_t[h:h + 1, :]
                decay = jnp.exp(jnp.where(causal, seg, -jnp.inf))
                ys.append(_dot((cb * decay).astype(BF16), x_pair))
            y_diag = jnp.where(lane < SSM_HEADDIM, ys[0], ys[1])
            y_ref[:, ps] = y_diag + y_off[:, pr * LANES:(pr + 1) * LANES] + dx_ref[:, ps] * xs[:, ps]

    yz = y_ref[...] * _silu(z_ref[0])
    for g in range(SSM_GROUPS):
        gs = slice(g * GROUP_W, (g + 1) * GROUP_W)
        blk = yz[:, gs]
        ms = jnp.mean(blk * blk, axis=-1, keepdims=True)
        o_ref[0, :, gs] = (blk * lax.rsqrt(ms + RMS_EPS) * ng_ref[:, gs]).astype(o_ref.dtype)


def _ssd_mixer(xbc, z, dt_raw, conv_w, conv_b, dt_bias, a_log, d_skip, norm_g):
    b, s, _ = xbc.shape
    L = SSM_CHUNK
    pad = LANES - SSM_HEADS
    dtb = jnp.pad(dt_bias, (0, pad)).reshape(1, LANES)
    alog = jnp.pad(a_log, (0, pad)).reshape(1, LANES)
    d_x = jnp.repeat(d_skip, SSM_HEADDIM).reshape(1, SSM_D_INNER)
    row = lambda n: pl.BlockSpec((1, n), lambda i, j: (0, 0))
    return pl.pallas_call(
        _ssd_kernel,
        grid=(b, s // L),
        in_specs=[pl.BlockSpec((1, L, SSM_XBC), lambda i, j: (i, j, 0)),
                  pl.BlockSpec((1, L, SSM_D_INNER), lambda i, j: (i, j, 0)),
                  pl.BlockSpec((1, L, LANES), lambda i, j: (i, j, 0)),
                  pl.BlockSpec((SSM_CONV, SSM_XBC), lambda i, j: (0, 0)),
                  row(SSM_XBC), row(LANES), row(LANES), row(SSM_D_INNER), row(SSM_D_INNER)],
        out_specs=pl.BlockSpec((1, L, SSM_D_INNER), lambda i, j: (i, j, 0)),
        out_shape=jax.ShapeDtypeStruct((b, s, SSM_D_INNER), BF16),
        scratch_shapes=[pltpu.VMEM((SUBLANES + L, SSM_XBC), F32),
                        pltpu.VMEM((SSM_GROUPS, SSM_STATE, GROUP_W), F32),
                        pltpu.VMEM((L, SSM_D_INNER), F32)],
        compiler_params=_cparams(("arbitrary", "arbitrary")),
        name="ssd_mixer",
    )(xbc, z, dt_raw, conv_w, conv_b.reshape(1, SSM_XBC), dtb, alog, d_x,
      norm_g.reshape(1, SSM_D_INNER))


DIL_MERGE_ROWS = 512


def _dil_kernel(q_ref, k_ref, v_ref, bias_ref, o_ref, qkv_ref, ob_ref, lb_ref, *, seq):
    blk = DIL_BLOCK
    sub = lax.broadcasted_iota(jnp.int32, (LANES, 1), 0)
    lane = lax.broadcasted_iota(jnp.int32, (1, LANES), 1)
    qkv_ref[0] = q_ref[0].astype(F32)
    qkv_ref[1] = k_ref[0].astype(F32)
    qkv_ref[2] = v_ref[0].astype(F32)

    def vt_aug(v):
        vt = v.T
        out = []
        for hh in range(PAIR):
            in_head = (sub >= hh * HEAD_DIM) & (sub < (hh + 1) * HEAD_DIM)
            ones_row = sub == (1 - hh) * HEAD_DIM
            out.append(jnp.where(in_head, vt, jnp.where(ones_row, 1.0, 0.0)).astype(BF16))
        return out

    for br, (_, dil) in enumerate(DIL_BRANCHES):
        nb = seq // dil // blk
        shift = nb.bit_length() - 1

        def rows(idx, dil=dil, nb=nb, shift=shift):
            start = (idx & (nb - 1)) * (blk * dil) + (idx >> shift)
            if dil == 1:
                return pl.ds(pl.multiple_of(start, blk), blk)
            return pl.ds(start, blk, stride=dil)

        def trip(t, carry, br=br, rows=rows, nb=nb):
            k_prev, vt_prev = carry[0], (carry[1], carry[2])
            work = []
            for u in range(DIL_UNROLL):
                idx = t * DIL_UNROLL + u
                first = (idx & (nb - 1)) == 0
                q = qkv_ref[0, rows(idx), :].astype(BF16)
                k_cur = qkv_ref[1, rows(idx), :].astype(BF16)
                for hh in range(PAIR):
                    in_head = (lane >= hh * HEAD_DIM) & (lane < (hh + 1) * HEAD_DIM)
                    qh = jnp.where(in_head, q, jnp.zeros_like(q)) * (HEAD_DIM ** -0.5)
                    work.append((first, idx, hh, _dot_nt(k_prev, qh), _dot_nt(k_cur, qh)))
                k_prev = k_cur
            soft = []
            for first, idx, hh, s_prev, s_cur in work:
                s_prev = jnp.where(first, NEG, s_prev + bias_ref[hh, br, 0:blk, :])
                s_cur = s_cur + bias_ref[hh, br, blk:2 * blk, :]
                m = jnp.maximum(jnp.max(s_prev, axis=0, keepdims=True), jnp.max(s_cur, axis=0, keepdims=True))
                soft.append((m, jnp.exp(s_prev - m).astype(BF16), jnp.exp(s_cur - m).astype(BF16)))
            res = []
            for u in range(DIL_UNROLL):
                idx = work[u * PAIR][1]
                vt_cur = vt_aug(qkv_ref[2, rows(idx), :])
                for hh in range(PAIR):
                    m, e_prev, e_cur = soft[u * PAIR + hh]
                    res.append((m, _dot(vt_prev[hh], e_prev) + _dot(vt_cur[hh], e_cur)))
                vt_prev = vt_cur
            for u in range(DIL_UNROLL):
                idx = work[u * PAIR][1]
                outs, lses = [], []
                for hh in range(PAIR):
                    m, acc = res[u * PAIR + hh]
                    den = acc[(1 - hh) * HEAD_DIM:(1 - hh) * HEAD_DIM + 1, :]
                    outs.append(acc / den)
                    lses.append(m + jnp.log(den))
                ob_ref[br, rows(idx), :] = jnp.where(sub < HEAD_DIM, outs[0], outs[1]).T
                lb_ref[br, rows(idx), :] = jnp.where(sub < HEAD_DIM, lses[0], lses[1]).T
            return k_prev, vt_prev[0], vt_prev[1]

        vt_first = vt_aug(qkv_ref[2, 0:blk, :])
        init = (qkv_ref[1, 0:blk, :].astype(BF16), vt_first[0], vt_first[1])
        lax.fori_loop(0, seq // blk // DIL_UNROLL, trip, init)

    def merge(i, carry):
        r = pl.ds(pl.multiple_of(i * DIL_MERGE_ROWS, DIL_MERGE_ROWS), DIL_MERGE_ROWS)
        lses = [lb_ref[b, r, :] for b in range(len(DIL_BRANCHES))]
        m = functools.reduce(jnp.maximum, lses)
        ws = [jnp.exp(l - m) for l in lses]
        num = sum(w * ob_ref[b, r, :] for b, w in enumerate(ws))
        o_ref[0, r, :] = (num / sum(ws)).astype(o_ref.dtype)
        return carry

    lax.fori_loop(0, seq // DIL_MERGE_ROWS, merge, 0)


def _dilated_attention(q, k, v, bias):
    b, s, w = q.shape
    npair = w // LANES
    nbr = len(DIL_BRANCHES)
    for _, dil in DIL_BRANCHES:
        nb = s // dil // DIL_BLOCK
        assert nb * dil * DIL_BLOCK == s and nb % DIL_UNROLL == 0 and nb & (nb - 1) == 0, (s, dil)
    assert s % DIL_MERGE_ROWS == 0
    spec = pl.BlockSpec((1, s, LANES), lambda i, p: (i, 0, p))
    return pl.pallas_call(
        functools.partial(_dil_kernel, seq=s),
        grid=(b, npair),
        in_specs=[spec, spec, spec,
                  pl.BlockSpec((PAIR, nbr, 2 * DIL_BLOCK, DIL_BLOCK), lambda i, p: (p, 0, 0, 0))],
        out_specs=spec,
        out_shape=jax.ShapeDtypeStruct((b, s, w), BF16),
        scratch_shapes=[pltpu.VMEM((3, s, LANES), F32),
                        pltpu.VMEM((nbr, s, LANES), F32),
                        pltpu.VMEM((nbr, s, LANES), F32)],
        compiler_params=_cparams(("arbitrary", "arbitrary")),
        name="dilated_attention",
    )(q, k, v, bias)


def _outproj_ln_kernel(*refs, n_in, alpha):
    a_refs, w_refs = refs[:n_in], refs[n_in:2 * n_in]
    x_ref, g_ref, b_ref, o_ref = refs[2 * n_in:]
    acc = _dot(a_refs[0][...], w_refs[0][...])
    for a_ref, w_ref in zip(a_refs[1:], w_refs[1:]):
        acc = acc + _dot(a_ref[...], w_ref[...])
    o_ref[...] = _layer_norm(alpha * x_ref[...] + acc, g_ref[...], b_ref[...])


def _outproj_ln(acts, weights, x2, g, b, alpha, tm):
    t, d = x2.shape
    n_in = len(acts)
    in_specs = ([pl.BlockSpec((tm, a.shape[1]), lambda i: (i, 0)) for a in acts]
                + [pl.BlockSpec(w.shape, lambda i: (0, 0)) for w in weights]
                + [pl.BlockSpec((tm, d), lambda i: (i, 0)),
                   pl.BlockSpec((1, d), lambda i: (0, 0)), pl.BlockSpec((1, d), lambda i: (0, 0))])
    return pl.pallas_call(
        functools.partial(_outproj_ln_kernel, n_in=n_in, alpha=alpha),
        grid=(t // tm,),
        in_specs=in_specs,
        out_specs=pl.BlockSpec((tm, d), lambda i: (i, 0)),
        out_shape=jax.ShapeDtypeStruct((t, d), F32),
        compiler_params=_cparams(("parallel",)),
        name="out_proj_ln",
    )(*acts, *weights, x2, g.reshape(1, d), b.reshape(1, d))


def _ffn_kernel(x_ref, wg_ref, wu_ref, wd_ref, g_ref, b_ref, o_ref, acc_ref, *, alpha):
    f = pl.program_id(1)

    @pl.when(f == 0)
    def _():
        acc_ref[...] = jnp.zeros(acc_ref.shape, F32)

    xb = x_ref[...].astype(BF16)
    gate = _dot(xb, wg_ref[...])
    up = _dot(xb, wu_ref[...])
    acc_ref[...] += _dot((_silu(gate) * up).astype(BF16), wd_ref[...])

    @pl.when(f == pl.num_programs(1) - 1)
    def _():
        o_ref[...] = _layer_norm(alpha * x_ref[...] + acc_ref[...], g_ref[...], b_ref[...])


def _ffn_ln(x2, w_in, w_down, g, b, alpha, tm, tf):
    t, d = x2.shape
    ff = w_down.shape[0]
    nf = ff // tf
    return pl.pallas_call(
        functools.partial(_ffn_kernel, alpha=alpha),
        grid=(t // tm, nf),
        in_specs=[pl.BlockSpec((tm, d), lambda i, f: (i, 0)),
                  pl.BlockSpec((d, tf), lambda i, f: (0, f)),
                  pl.BlockSpec((d, tf), lambda i, f: (0, f + nf)),
                  pl.BlockSpec((tf, d), lambda i, f: (f, 0)),
                  pl.BlockSpec((1, d), lambda i, f: (0, 0)),
                  pl.BlockSpec((1, d), lambda i, f: (0, 0))],
        out_specs=pl.BlockSpec((tm, d), lambda i, f: (i, 0)),
        out_shape=jax.ShapeDtypeStruct((t, d), F32),
        scratch_shapes=[pltpu.VMEM((tm, d), F32)],
        compiler_params=_cparams(("parallel", "arbitrary")),
        name="ffn_ln",
    )(x2, w_in, w_in, w_down, g.reshape(1, d), b.reshape(1, d))


MOE_TM = 1024
MOE_SUB = 64


def _router_kernel(x_ref, wr_ref, c_ref, pos_ref, post_ref, cnt_ref):
    tm = x_ref.shape[0]
    logits = _dot(x_ref[...], wr_ref[...], precision=HIGHEST)
    lanef = lax.broadcasted_iota(jnp.int32, logits.shape, 1).astype(F32)
    cur = jnp.where(lanef < N_EXPERTS, logits, -jnp.inf)
    vals, hits = [], []
    for _ in range(MOE_TOPK):
        best = jnp.max(cur, axis=1, keepdims=True)
        first = jnp.min(jnp.where(cur == best, lanef, float(LANES)), axis=1, keepdims=True)
        hit = lanef == first
        vals.append(best)
        hits.append(hit)
        cur = jnp.where(hit, -jnp.inf, cur)
    es = [jnp.exp(v - vals[0]) for v in vals]
    den = sum(es)
    comb = jnp.zeros(logits.shape, F32)
    routed = jnp.zeros(logits.shape, F32)
    for e, hit in zip(es, hits):
        comb = jnp.where(hit, e / den, comb)
        routed = jnp.where(hit, 1.0, routed)
    c_ref[...] = comb
    r = lax.broadcasted_iota(jnp.int32, (tm, tm), 0)
    c = lax.broadcasted_iota(jnp.int32, (tm, tm), 1)
    incl = _dot(jnp.where(c <= r, 1.0, 0.0).astype(BF16), routed.astype(BF16))
    pos = jnp.where(routed > 0.5, incl - 1.0, -1.0)
    pos_ref[...] = pos
    post_ref[0] = pos.T[:SUBLANES, :]
    cnt_ref[...] = jnp.broadcast_to(incl[tm - 1:tm, :], cnt_ref.shape)


def _router(x2, w_router, tm):
    t, d = x2.shape
    nt = t // tm
    wr = jnp.pad(w_router, ((0, 0), (0, LANES - N_EXPERTS)))
    return pl.pallas_call(
        _router_kernel,
        grid=(nt,),
        in_specs=[pl.BlockSpec((tm, d), lambda i: (i, 0)),
                  pl.BlockSpec((d, LANES), lambda i: (0, 0))],
        out_specs=[pl.BlockSpec((tm, LANES), lambda i: (i, 0)),
                   pl.BlockSpec((tm, LANES), lambda i: (i, 0)),
                   pl.BlockSpec((1, SUBLANES, tm), lambda i: (i, 0, 0)),
                   pl.BlockSpec((SUBLANES, LANES), lambda i: (i, 0))],
        out_shape=[jax.ShapeDtypeStruct((t, LANES), F32),
                   jax.ShapeDtypeStruct((t, LANES), F32),
                   jax.ShapeDtypeStruct((nt, SUBLANES, tm), F32),
                   jax.ShapeDtypeStruct((nt * SUBLANES, LANES), F32)],
        compiler_params=_cparams(("parallel",)),
        name="moe_router",
    )(x2, wr)


def _moe_kernel(nsub_ref, x_ref, c_ref, pos_ref, post_ref, wg_ref, wu_ref, wd_ref, g_ref, b_ref, o_ref,
                xb_ref, xs_ref, gs_ref, oe_ref, *, alpha):
    i = pl.program_id(0)
    e = pl.program_id(1)
    f = pl.program_id(2)
    last_f = pl.num_programs(2) - 1
    tm = x_ref.shape[0]
    sub = MOE_SUB
    n = nsub_ref[i * N_EXPERTS + e]
    lane = lax.broadcasted_iota(jnp.int32, (1, LANES), 1)

    @pl.when((e == 0) & (f == 0))
    def _():
        o_ref[...] = jnp.zeros(o_ref.shape, F32)
        xb_ref[...] = x_ref[...].astype(BF16)

    def for_subtiles(fn):
        def quad(s, carry):
            fn(pl.multiple_of(s * (4 * sub), 4 * sub), 4 * sub)
            return carry

        nq = n >> 2
        lax.fori_loop(0, nq, quad, 0)

        @pl.when((n & 2) != 0)
        def _():
            fn(pl.multiple_of(nq * (4 * sub), 2 * sub), 2 * sub)

        @pl.when((n & 1) != 0)
        def _():
            fn(pl.multiple_of((n - 1) * sub, sub), sub)

    @pl.when(f == 0)
    def _():
        comb = c_ref[...]
        c_hi = comb.astype(BF16)
        c_lo = (comb - c_hi.astype(F32)).astype(BF16)
        prow = post_ref[0, pl.ds(e, 1), :]

        def gather(r0, rows):
            want = (lax.broadcasted_iota(jnp.int32, (rows, tm), 0) + r0).astype(F32)
            sel = jnp.where(prow == want, 1.0, 0.0).astype(BF16)
            xs_ref[pl.ds(r0, rows), :] = _dot(sel, xb_ref[...]).astype(BF16)
            cg = _dot(sel, c_hi) + _dot(sel, c_lo)
            ce = jnp.sum(jnp.where(lane == e, cg, 0.0), axis=1, keepdims=True)
            gs_ref[pl.ds(r0, rows), :] = jnp.broadcast_to(ce, (rows, LANES))

        for_subtiles(gather)

    def ffn(r0, rows):
        xs = xs_ref[pl.ds(r0, rows), :]
        gate = _dot(xs, wg_ref[0])
        up = _dot(xs, wu_ref[0])
        act = (_silu(gate) * up * gs_ref[pl.ds(r0, rows), 0:1]).astype(BF16)
        contrib = _dot(act, wd_ref[0])

        @pl.when(f == 0)
        def _():
            oe_ref[pl.ds(r0, rows), :] = contrib

        @pl.when(f > 0)
        def _():
            oe_ref[pl.ds(r0, rows), :] += contrib

    for_subtiles(ffn)

    @pl.when(f == last_f)
    def _():
        pos = pos_ref[...]
        pcol = jnp.sum(jnp.where(lane == e, pos, 0.0), axis=1, keepdims=True)

        def scatter(r0, rows):
            want = (lax.broadcasted_iota(jnp.int32, (tm, rows), 1) + r0).astype(F32)
            sel_t = jnp.where(pcol == want, 1.0, 0.0).astype(BF16)
            o_ref[...] += _dot(sel_t, oe_ref[pl.ds(r0, rows), :].astype(BF16))

        for_subtiles(scatter)

    @pl.when((e == pl.num_programs(1) - 1) & (f == last_f))
    def _():
        o_ref[...] = _layer_norm(alpha * x_ref[...] + o_ref[...], g_ref[...], b_ref[...])


def _moe_ln(x2, routing, w_in, w_down, g, b, alpha, tm, tf):
    comb, pos, pos_t, cnt = routing
    t, d = x2.shape
    ne, ff, _ = w_down.shape
    nf = ff // tf
    nt = t // tm
    counts = cnt.reshape(nt, SUBLANES, LANES)[:, 0, :ne]
    nsub = jnp.ceil(counts / MOE_SUB).astype(jnp.int32).reshape(nt * ne)
    grid_spec = pltpu.PrefetchScalarGridSpec(
        num_scalar_prefetch=1,
        grid=(nt, ne, nf),
        in_specs=[pl.BlockSpec((tm, d), lambda i, e, f, ns: (i, 0)),
                  pl.BlockSpec((tm, LANES), lambda i, e, f, ns: (i, 0)),
                  pl.BlockSpec((tm, LANES), lambda i, e, f, ns: (i, 0)),
                  pl.BlockSpec((1, SUBLANES, tm), lambda i, e, f, ns: (i, 0, 0)),
                  pl.BlockSpec((1, d, tf), lambda i, e, f, ns: (e, 0, f)),
                  pl.BlockSpec((1, d, tf), lambda i, e, f, ns: (e, 0, f + nf)),
                  pl.BlockSpec((1, tf, d), lambda i, e, f, ns: (e, f, 0)),
                  pl.BlockSpec((1, d), lambda i, e, f, ns: (0, 0)),
                  pl.BlockSpec((1, d), lambda i, e, f, ns: (0, 0))],
        out_specs=pl.BlockSpec((tm, d), lambda i, e, f, ns: (i, 0)),
        scratch_shapes=[pltpu.VMEM((tm, d), BF16),
                        pltpu.VMEM((tm, d), BF16),
                        pltpu.VMEM((tm, LANES), F32),
                        pltpu.VMEM((tm, d), F32)])
    return pl.pallas_call(
        functools.partial(_moe_kernel, alpha=alpha),
        grid_spec=grid_spec,
        out_shape=jax.ShapeDtypeStruct((t, d), F32),
        compiler_params=pltpu.CompilerParams(dimension_semantics=("parallel", "arbitrary", "arbitrary"),
                                             vmem_limit_bytes=MOE_VMEM_LIMIT_BYTES),
        name="moe_ln",
    )(nsub, x2, comb, pos, pos_t, w_in, w_in, w_down, g.reshape(1, d), b.reshape(1, d))


def _tile(t, want):
    while t % want:
        want //= 2
    return want


def _even_layer(x2, bsz, seq, w_in, conv_w, conv_b, dt_bias, a_log, d_skip, norm_g, w_out,
                moba_bias, ln_g, ln_b, alpha):
    t = x2.shape[0]
    w_cat = jnp.concatenate(
        [w_in[:, :EVEN_MAIN], jnp.pad(w_in[:, EVEN_MAIN:], ((0, 0), (0, LANES - SSM_HEADS)))], axis=1
    ).astype(BF16)
    bounds = (0, MOBA_W, 2 * MOBA_W, 3 * MOBA_W, 3 * MOBA_W + SSM_D_INNER, EVEN_MAIN, EVEN_MAIN + LANES)
    segs = tuple(zip(bounds[:-1], bounds[1:]))
    q, k, v, z, xbc, dt_raw = _project(x2, w_cat, segs, (BF16, BF16, BF16, F32, F32, F32), _tile(t, 256))
    r3 = lambda a: a.reshape(bsz, seq, a.shape[-1])
    attn = _moba_attention(r3(q), r3(k), r3(v), moba_bias).reshape(t, MOBA_W)
    ssm = _ssd_mixer(r3(xbc), r3(z), r3(dt_raw), conv_w, conv_b, dt_bias, a_log, d_skip,
                     norm_g).reshape(t, SSM_D_INNER)
    w_o = w_out.astype(BF16)
    return _outproj_ln([attn, ssm], [w_o[:MOBA_W], w_o[MOBA_W:]], x2, ln_g, ln_b, alpha, _tile(t, 512))


def _odd_layer(x2, bsz, seq, w_qkv, w_out, dil_bias, ln_g, ln_b, alpha):
    t = x2.shape[0]
    w = DIL_HEADS * HEAD_DIM
    segs = ((0, w), (w, 2 * w), (2 * w, 3 * w))
    q, k, v = _project(x2, w_qkv.astype(BF16), segs, (BF16, BF16, BF16), _tile(t, 256))
    r3 = lambda a: a.reshape(bsz, seq, w)
    merged = _dilated_attention(r3(q), r3(k), r3(v), dil_bias).reshape(t, w)
    return _outproj_ln([merged], [w_out.astype(BF16)], x2, ln_g, ln_b, alpha, _tile(t, 512))


def kernel(x, rel_bias, ln_mix_g, ln_mix_b, ln_ffn_g, ln_ffn_b, even_w_in, even_conv_w, even_conv_b,
           even_dt_bias, even_A_log, even_D, even_norm_g, even_w_out, ffn_w_in, ffn_w_down,
           odd_w_qkv, odd_w_out, moe_router, moe_w_in, moe_w_down):
    bsz, seq, d = x.shape
    depth = ln_mix_g.shape[0]
    alpha = (2.0 * depth) ** 0.25
    t = bsz * seq
    x2 = x.reshape(t, d)
    moba_bias = _bias_tiles(rel_bias, _moba_bucket_idx(seq // MOBA_BLOCK), MOBA_HEADS)
    dil_bias = _bias_tiles(rel_bias, _dil_bucket_idx(), DIL_HEADS)
    for l in range(depth):
        e = l // 2
        if l % 2 == 0:
            x2 = _even_layer(x2, bsz, seq, even_w_in[e], even_conv_w[e], even_conv_b[e], even_dt_bias[e],
                             even_A_log[e], even_D[e], even_norm_g[e], even_w_out[e], moba_bias,
                             ln_mix_g[l], ln_mix_b[l], alpha)
            x2 = _ffn_ln(x2, ffn_w_in[e].astype(BF16), ffn_w_down[e].astype(BF16), ln_ffn_g[l], ln_ffn_b[l],
                         alpha, _tile(t, 512), FFN_DENSE // 2)
        else:
            x2 = _odd_layer(x2, bsz, seq, odd_w_qkv[e], odd_w_out[e], dil_bias, ln_mix_g[l], ln_mix_b[l], alpha)
            tm = _tile(t, MOE_TM)
            routing = _router(x2, moe_router[e], tm)
            x2 = _moe_ln(x2, routing, moe_w_in[e].astype(BF16), moe_w_down[e].astype(BF16), ln_ffn_g[l],
                         ln_ffn_b[l], alpha, tm, FFN_EXPERT // 4)
    return x2.reshape(bsz, seq, d)
```

```python
import functools
import math

import jax
import jax.numpy as jnp
from jax import lax
from jax.experimental import pallas as pl
from jax.experimental.pallas import tpu as pltpu

F32 = jnp.float32
BF16 = jnp.bfloat16
HIGHEST = lax.Precision.HIGHEST

LANES = 128
SUBLANES = 8
VMEM_LIMIT_BYTES = 48 * 1024 * 1024
MOE_VMEM_LIMIT_BYTES = 54 * 1024 * 1024

D_MODEL = 1024
HEAD_DIM = 64
MOBA_HEADS = 8
MOBA_BLOCK = 256
MOBA_TOPK = 3
MOBA_UNROLL = 4
SSM_D_INNER = D_MODEL
SSM_HEADDIM = 64
SSM_HEADS = SSM_D_INNER // SSM_HEADDIM
SSM_GROUPS = 4
SSM_STATE = 128
SSM_CONV = 4
SSM_CHUNK = 128
DIL_HEADS = D_MODEL // HEAD_DIM
DIL_BRANCHES = ((128, 1), (512, 4), (2048, 16))
DIL_BLOCK = 128
DIL_UNROLL = 2
REL_BUCKETS = 32
REL_MAX_DIST = 2048
REL_HEADS = max(MOBA_HEADS, DIL_HEADS)
FFN_DENSE = 2816
N_EXPERTS = 8
MOE_TOPK = 2
FFN_EXPERT = 3584
LN_EPS = 1e-5
RMS_EPS = 1e-5
MOBA_W = MOBA_HEADS * HEAD_DIM
SSM_BC = SSM_GROUPS * SSM_STATE
SSM_XBC = SSM_D_INNER + 2 * SSM_BC
EVEN_MAIN = 3 * MOBA_W + SSM_D_INNER + SSM_XBC
GROUP_W = SSM_D_INNER // SSM_GROUPS

NEG = -1e30
PAIR = LANES // HEAD_DIM


def _cparams(sem):
    return pltpu.CompilerParams(dimension_semantics=sem, vmem_limit_bytes=VMEM_LIMIT_BYTES)


def _dot(a, b, **kw):
    return jnp.dot(a, b, preferred_element_type=F32, **kw)


def _dot_nt(a, b):
    return lax.dot_general(a, b, (((1,), (1,)), ((), ())), preferred_element_type=F32)


def _silu(x):
    return x * jax.nn.sigmoid(x)


def _layer_norm(y, g, b):
    mu = jnp.mean(y, axis=-1, keepdims=True)
    yc = y - mu
    var = jnp.mean(yc * yc, axis=-1, keepdims=True)
    return yc * lax.rsqrt(var + LN_EPS) * g + b


def _bias_tile_kernel(rb_ref, idx_ref, o_ref):
    h = pl.program_id(0)
    idx = idx_ref[0]
    acc = jnp.zeros(idx.shape, F32)
    for b in range(REL_BUCKETS):
        acc = jnp.where(idx == b, rb_ref[b, h], acc)
    o_ref[0, 0] = jnp.where(idx < 0, NEG, acc)


def _bias_tiles(rel_bias, idx, heads):
    n, r, c = idx.shape
    return pl.pallas_call(
        _bias_tile_kernel,
        grid=(heads, n),
        in_specs=[pl.BlockSpec(memory_space=pltpu.SMEM),
                  pl.BlockSpec((1, r, c), lambda h, i: (i, 0, 0))],
        out_specs=pl.BlockSpec((1, 1, r, c), lambda h, i: (h, i, 0, 0)),
        out_shape=jax.ShapeDtypeStruct((heads, n, r, c), F32),
        compiler_params=_cparams(("arbitrary", "arbitrary")),
        name="bias_tiles",
    )(rel_bias, idx)


def _rel_bucket(dist):
    max_exact = REL_BUCKETS // 2
    d = jnp.maximum(dist, 0)
    df = jnp.maximum(d, 1).astype(F32)
    large = max_exact + (jnp.log(df / max_exact) / math.log(REL_MAX_DIST / max_exact)
                         * (REL_BUCKETS - max_exact)).astype(jnp.int32)
    large = jnp.minimum(large, REL_BUCKETS - 1)
    return jnp.where(d < max_exact, d, large)


def _moba_bucket_idx(nblk):
    j = jnp.arange(MOBA_BLOCK, dtype=jnp.int32)[:, None]
    i = jnp.arange(MOBA_BLOCK, dtype=jnp.int32)[None, :]
    delta = jnp.arange(nblk, dtype=jnp.int32)[:, None, None]
    dist = delta * MOBA_BLOCK + i - j
    return jnp.where(dist >= 0, _rel_bucket(dist), -1)


def _dil_bucket_idx():
    ki = jnp.arange(2 * DIL_BLOCK, dtype=jnp.int32)[:, None]
    qi = jnp.arange(DIL_BLOCK, dtype=jnp.int32)[None, :]
    steps = qi + DIL_BLOCK - ki
    tiles = []
    for window, dil in DIL_BRANCHES:
        span = window // dil
        valid = (steps >= 0) & (steps <= span)
        tiles.append(jnp.where(valid, _rel_bucket(steps * dil), -1))
    return jnp.stack(tiles)


def _proj_kernel(x_ref, w_ref, *o_refs, segs):
    xb = x_ref[...].astype(BF16)
    for o_ref, (c0, c1) in zip(o_refs, segs):
        o_ref[...] = _dot(xb, w_ref[:, c0:c1]).astype(o_ref.dtype)


def _project(x2, w_bf16, segs, dtypes, tm):
    t, d = x2.shape
    n = w_bf16.shape[1]
    return pl.pallas_call(
        functools.partial(_proj_kernel, segs=segs),
        grid=(t // tm,),
        in_specs=[pl.BlockSpec((tm, d), lambda i: (i, 0)),
                  pl.BlockSpec((d, n), lambda i: (0, 0))],
        out_specs=[pl.BlockSpec((tm, c1 - c0), lambda i: (i, 0)) for c0, c1 in segs],
        out_shape=[jax.ShapeDtypeStruct((t, c1 - c0), dt) for (c0, c1), dt in zip(segs, dtypes)],
        compiler_params=_cparams(("parallel",)),
        name="in_proj",
    )(x2, w_bf16)


def _moba_kernel(q_ref, k_ref, v_ref, bias_ref, o_ref, kmean_ref, vt_ref, mask_ref, *, nblk):
    qi = pl.program_id(2)
    blk = MOBA_BLOCK
    seq = k_ref.shape[1]
    lane = lax.broadcasted_iota(jnp.int32, (1, LANES), 1)
    sub = lax.broadcasted_iota(jnp.int32, (LANES, 1), 0)

    @pl.when(qi == 0)
    def _():
        r = lax.broadcasted_iota(jnp.int32, (nblk, seq), 0)
        c = lax.broadcasted_iota(jnp.int32, (nblk, seq), 1)
        avg = jnp.where((c >= r * blk) & (c < (r + 1) * blk), 1.0 / blk, 0.0).astype(BF16)
        kmean_ref[...] = _dot(avg, k_ref[0])

        def build(j, carry):
            r0 = pl.multiple_of(j * blk, blk)
            vt = v_ref[0, pl.ds(r0, blk), :].astype(F32).T
            for hh in range(PAIR):
                in_head = (sub >= hh * HEAD_DIM) & (sub < (hh + 1) * HEAD_DIM)
                ones_row = sub == (1 - hh) * HEAD_DIM
                vt_ref[hh, j] = jnp.where(in_head, vt, jnp.where(ones_row, 1.0, 0.0)).astype(BF16)
            return carry

        lax.fori_loop(0, nblk, build, 0)

    q = q_ref[0]
    kmean = kmean_ref[...]
    bf = lax.broadcasted_iota(jnp.int32, (nblk, blk), 0).astype(F32)
    qif = qi.astype(F32)

    qhs = []
    for hh in range(PAIR):
        in_head = (lane >= hh * HEAD_DIM) & (lane < (hh + 1) * HEAD_DIM)
        km = jnp.where(in_head, kmean, 0.0)
        km_hi = km.astype(BF16)
        km_lo = (km - km_hi.astype(F32)).astype(BF16)
        gate = _dot_nt(km_hi, q) + _dot_nt(km_lo, q)
        gate = jnp.where(bf < qif, gate, -jnp.inf)
        sel = jnp.where(bf == qif, 1.0, 0.0)
        for _ in range(MOBA_TOPK):
            best = jnp.max(gate, axis=0, keepdims=True)
            first = jnp.min(jnp.where(gate == best, bf, float(nblk)), axis=0, keepdims=True)
            hit = bf == first
            sel = jnp.where(hit & (best > -jnp.inf), 1.0, sel)
            gate = jnp.where(hit, -jnp.inf, gate)
        mask_ref[hh] = jnp.where(sel > 0.5, 0.0, NEG)
        qhs.append(jnp.where(in_head, q, jnp.zeros_like(q)) * (HEAD_DIM ** -0.5))

    def body(t, carry):
        j0 = t * MOBA_UNROLL
        r0 = pl.multiple_of(j0 * blk, MOBA_UNROLL * blk)
        k_grp = k_ref[0, pl.ds(r0, MOBA_UNROLL * blk), :]
        s_alls = [_dot_nt(k_grp, qhs[hh]) for hh in range(PAIR)]
        soft = []
        for hh in range(PAIR):
            m = carry[2 * hh]
            parts = []
            for u in range(MOBA_UNROLL):
                j = j0 + u
                delta = jnp.maximum(qi - j, 0)
                parts.append(s_alls[hh][u * blk:(u + 1) * blk] + bias_ref[hh, delta]
                             + mask_ref[hh, pl.ds(j, 1), :])
            m_new = functools.reduce(jnp.maximum, [m] + [jnp.max(s, axis=0, keepdims=True) for s in parts])
            soft.append((m_new, jnp.exp(m - m_new), [jnp.exp(s - m_new).astype(BF16) for s in parts]))
        new = []
        for hh in range(PAIR):
            m_new, alpha, ps = soft[hh]
            acc = alpha * carry[2 * hh + 1]
            for u, p in enumerate(ps):
                acc = acc + _dot(vt_ref[hh, j0 + u], p)
            new += [m_new, acc]
        return tuple(new)

    m_init = jnp.full((1, blk), 4 * NEG, F32)
    acc_init = jnp.zeros((LANES, blk), F32)
    trips = (qi + MOBA_UNROLL) // MOBA_UNROLL
    fin = lax.fori_loop(0, trips, body, (m_init, acc_init, m_init, acc_init))
    outs = []
    for hh in range(PAIR):
        acc = fin[2 * hh + 1]
        den = acc[(1 - hh) * HEAD_DIM:(1 - hh) * HEAD_DIM + 1, :]
        outs.append(acc / den)
    o_t = jnp.where(sub < HEAD_DIM, outs[0], outs[1])
    o_ref[0] = o_t.T.astype(o_ref.dtype)


def _moba_attention(q, k, v, bias):
    b, s, w = q.shape
    nblk = s // MOBA_BLOCK
    assert s % MOBA_BLOCK == 0 and nblk % MOBA_UNROLL == 0, s
    npair = w // LANES
    return pl.pallas_call(
        functools.partial(_moba_kernel, nblk=nblk),
        grid=(npair, b, nblk),
        in_specs=[pl.BlockSpec((1, MOBA_BLOCK, LANES), lambda p, i, j: (i, j, p)),
                  pl.BlockSpec((1, s, LANES), lambda p, i, j: (i, 0, p)),
                  pl.BlockSpec((1, s, LANES), lambda p, i, j: (i, 0, p)),
                  pl.BlockSpec((PAIR, nblk, MOBA_BLOCK, MOBA_BLOCK), lambda p, i, j: (p, 0, 0, 0))],
        out_specs=pl.BlockSpec((1, MOBA_BLOCK, LANES), lambda p, i, j: (i, j, p)),
        out_shape=jax.ShapeDtypeStruct((b, s, w), BF16),
        scratch_shapes=[pltpu.VMEM((nblk, LANES), F32),
                        pltpu.VMEM((PAIR, nblk, LANES, MOBA_BLOCK), BF16),
                        pltpu.VMEM((PAIR, nblk, MOBA_BLOCK), F32)],
        compiler_params=_cparams(("arbitrary", "arbitrary", "arbitrary")),
        name="moba_attention",
    )(q, k, v, bias)


def _ssd_kernel(xbc_ref, z_ref, dt_ref, cw_ref, cb_ref, dtb_ref, alog_ref, dx_ref, ng_ref,
                o_ref, ext_ref, h_ref, y_ref):
    c = pl.program_id(1)
    L = SSM_CHUNK
    tail = SUBLANES

    @pl.when(c == 0)
    def _():
        ext_ref[0:tail, :] = jnp.zeros((tail, SSM_XBC), F32)
        h_ref[...] = jnp.zeros(h_ref.shape, F32)

    u = xbc_ref[0]
    ext_ref[tail:tail + L, :] = u
    conv = u * cw_ref[SSM_CONV - 1:SSM_CONV, :] + cb_ref[...]
    for j in range(1, SSM_CONV):
        conv = conv + ext_ref[tail - j:tail - j + L, :] * cw_ref[SSM_CONV - 1 - j:SSM_CONV - j, :]
    ext_ref[0:tail, :] = u[L - tail:L, :]
    xc = _silu(conv)
    xs = xc[:, :SSM_D_INNER]

    x_dt = dt_ref[0] + dtb_ref[...]
    dt = jnp.maximum(x_dt, 0.0) + jnp.log1p(jnp.exp(-jnp.abs(x_dt)))
    a = dt * (-jnp.exp(alog_ref[...]))
    row = lax.broadcasted_iota(jnp.int32, (L, L), 0)
    col = lax.broadcasted_iota(jnp.int32, (L, L), 1)
    causal = row >= col
    a_cs = _dot(jnp.where(causal, 1.0, 0.0), a, precision=HIGHEST)
    a_cs_t = a_cs.T
    er = lax.broadcasted_iota(jnp.int32, (LANES, SSM_D_INNER), 0)
    ec = lax.broadcasted_iota(jnp.int32, (LANES, SSM_D_INNER), 1)
    expand = jnp.where((ec >= er * SSM_HEADDIM) & (ec < (er + 1) * SSM_HEADDIM), 1.0, 0.0)
    a_cs_x = _dot(a_cs, expand, precision=HIGHEST)
    dt_x = _dot(dt, expand, precision=HIGHEST)
    a_last_x = a_cs_x[L - 1:L, :]
    xdt = xs * dt_x
    w_end = xdt * jnp.exp(a_last_x - a_cs_x)
    decay_in = jnp.exp(a_cs_x)
    chunk_decay = jnp.exp(a_last_x)
    lane = lax.broadcasted_iota(jnp.int32, (1, LANES), 1)
    hpg = SSM_HEADS // SSM_GROUPS

    for g in range(SSM_GROUPS):
        gs = slice(g * GROUP_W, (g + 1) * GROUP_W)
        b_g = xc[:, SSM_D_INNER + g * SSM_STATE:SSM_D_INNER + (g + 1) * SSM_STATE]
        c_g = xc[:, SSM_D_INNER + SSM_BC + g * SSM_STATE:SSM_D_INNER + SSM_BC + (g + 1) * SSM_STATE]
        b_bf = b_g.astype(BF16)
        c_bf = c_g.astype(BF16)
        cb = _dot_nt(c_bf, b_bf)
        h_in = h_ref[g]
        y_off = _dot(c_bf, h_in.astype(BF16)) * decay_in[:, gs]
        st = _dot(b_g.T.astype(BF16), w_end[:, gs].astype(BF16))
        h_ref[g] = h_in * chunk_decay[:, gs] + st
        for pr in range(hpg // PAIR):
            ps = slice(g * GROUP_W + pr * LANES, g * GROUP_W + (pr + 1) * LANES)
            x_pair = xdt[:, ps].astype(BF16)
            ys = []
            for hh in range(PAIR):
                h = g * hpg + pr * PAIR + hh
                seg = a_cs[:, h:h + 1] - a_cs_t[h:h + 1, :]
                decay = jnp.exp(jnp.where(causal, seg, -jnp.inf))
                ys.append(_dot((cb * decay).astype(BF16), x_pair))
            y_diag = jnp.where(lane < SSM_HEADDIM, ys[0], ys[1])
            y_ref[:, ps] = y_diag + y_off[:, pr * LANES:(pr + 1) * LANES] + dx_ref[:, ps] * xs[:, ps]

    yz = y_ref[...] * _silu(z_ref[0])
    for g in range(SSM_GROUPS):
        gs = slice(g * GROUP_W, (g + 1) * GROUP_W)
        blk = yz[:, gs]
        ms = jnp.mean(blk * blk, axis=-1, keepdims=True)
        o_ref[0, :, gs] = (blk * lax.rsqrt(ms + RMS_EPS) * ng_ref[:, gs]).astype(o_ref.dtype)


def _ssd_mixer(xbc, z, dt_raw, conv_w, conv_b, dt_bias, a_log, d_skip, norm_g):
    b, s, _ = xbc.shape
    L = SSM_CHUNK
    pad = LANES - SSM_HEADS
    dtb = jnp.pad(dt_bias, (0, pad)).reshape(1, LANES)
    alog = jnp.pad(a_log, (0, pad)).reshape(1, LANES)
    d_x = jnp.repeat(d_skip, SSM_HEADDIM).reshape(1, SSM_D_INNER)
    row = lambda n: pl.BlockSpec((1, n), lambda i, j: (0, 0))
    return pl.pallas_call(
        _ssd_kernel,
        grid=(b, s // L),
        in_specs=[pl.BlockSpec((1, L, SSM_XBC), lambda i, j: (i, j, 0)),
                  pl.BlockSpec((1, L, SSM_D_INNER), lambda i, j: (i, j, 0)),
                  pl.BlockSpec((1, L, LANES), lambda i, j: (i, j, 0)),
                  pl.BlockSpec((SSM_CONV, SSM_XBC), lambda i, j: (0, 0)),
                  row(SSM_XBC), row(LANES), row(LANES), row(SSM_D_INNER), row(SSM_D_INNER)],
        out_specs=pl.BlockSpec((1, L, SSM_D_INNER), lambda i, j: (i, j, 0)),
        out_shape=jax.ShapeDtypeStruct((b, s, SSM_D_INNER), BF16),
        scratch_shapes=[pltpu.VMEM((SUBLANES + L, SSM_XBC), F32),
                        pltpu.VMEM((SSM_GROUPS, SSM_STATE, GROUP_W), F32),
                        pltpu.VMEM((L, SSM_D_INNER), F32)],
        compiler_params=_cparams(("arbitrary", "arbitrary")),
        name="ssd_mixer",
    )(xbc, z, dt_raw, conv_w, conv_b.reshape(1, SSM_XBC), dtb, alog, d_x,
      norm_g.reshape(1, SSM_D_INNER))


DIL_MERGE_ROWS = 512


def _dil_kernel(q_ref, k_ref, v_ref, bias_ref, o_ref, qkv_ref, ob_ref, lb_ref, *, seq):
    blk = DIL_BLOCK
    sub = lax.broadcasted_iota(jnp.int32, (LANES, 1), 0)
    lane = lax.broadcasted_iota(jnp.int32, (1, LANES), 1)
    qkv_ref[0] = q_ref[0].astype(F32)
    qkv_ref[1] = k_ref[0].astype(F32)
    qkv_ref[2] = v_ref[0].astype(F32)

    def vt_aug(v):
        vt = v.T
        out = []
        for hh in range(PAIR):
            in_head = (sub >= hh * HEAD_DIM) & (sub < (hh + 1) * HEAD_DIM)
            ones_row = sub == (1 - hh) * HEAD_DIM
            out.append(jnp.where(in_head, vt, jnp.where(ones_row, 1.0, 0.0)).astype(BF16))
        return out

    for br, (_, dil) in enumerate(DIL_BRANCHES):
        nb = seq // dil // blk
        shift = nb.bit_length() - 1

        def rows(idx, dil=dil, nb=nb, shift=shift):
            start = (idx & (nb - 1)) * (blk * dil) + (idx >> shift)
            if dil == 1:
                return pl.ds(pl.multiple_of(start, blk), blk)
            return pl.ds(start, blk, stride=dil)

        def trip(t, carry, br=br, rows=rows, nb=nb):
            k_prev, vt_prev = carry[0], (carry[1], carry[2])
            work = []
            for u in range(DIL_UNROLL):
                idx = t * DIL_UNROLL + u
                first = (idx & (nb - 1)) == 0
                q = qkv_ref[0, rows(idx), :].astype(BF16)
                k_cur = qkv_ref[1, rows(idx), :].astype(BF16)
                for hh in range(PAIR):
                    in_head = (lane >= hh * HEAD_DIM) & (lane < (hh + 1) * HEAD_DIM)
                    qh = jnp.where(in_head, q, jnp.zeros_like(q)) * (HEAD_DIM ** -0.5)
                    work.append((first, idx, hh, _dot_nt(k_prev, qh), _dot_nt(k_cur, qh)))
                k_prev = k_cur
            soft = []
            for first, idx, hh, s_prev, s_cur in work:
                s_prev = jnp.where(first, NEG, s_prev + bias_ref[hh, br, 0:blk, :])
                s_cur = s_cur + bias_ref[hh, br, blk:2 * blk, :]
                m = jnp.maximum(jnp.max(s_prev, axis=0, keepdims=True), jnp.max(s_cur, axis=0, keepdims=True))
                soft.append((m, jnp.exp(s_prev - m).astype(BF16), jnp.exp(s_cur - m).astype(BF16)))
            res = []
            for u in range(DIL_UNROLL):
                idx = work[u * PAIR][1]
                vt_cur = vt_aug(qkv_ref[2, rows(idx), :])
                for hh in range(PAIR):
                    m, e_prev, e_cur = soft[u * PAIR + hh]
                    res.append((m, _dot(vt_prev[hh], e_prev) + _dot(vt_cur[hh], e_cur)))
                vt_prev = vt_cur
            for u in range(DIL_UNROLL):
                idx = work[u * PAIR][1]
                outs, lses = [], []
                for hh in range(PAIR):
                    m, acc = res[u * PAIR + hh]
                    den = acc[(1 - hh) * HEAD_DIM:(1 - hh) * HEAD_DIM + 1, :]
                    outs.append(acc / den)
                    lses.append(m + jnp.log(den))
                ob_ref[br, rows(idx), :] = jnp.where(sub < HEAD_DIM, outs[0], outs[1]).T
                lb_ref[br, rows(idx), :] = jnp.where(sub < HEAD_DIM, lses[0], lses[1]).T
            return k_prev, vt_prev[0], vt_prev[1]

        vt_first = vt_aug(qkv_ref[2, 0:blk, :])
        init = (qkv_ref[1, 0:blk, :].astype(BF16), vt_first[0], vt_first[1])
        lax.fori_loop(0, seq // blk // DIL_UNROLL, trip, init)

    def merge(i, carry):
        r = pl.ds(pl.multiple_of(i * DIL_MERGE_ROWS, DIL_MERGE_ROWS), DIL_MERGE_ROWS)
        lses = [lb_ref[b, r, :] for b in range(len(DIL_BRANCHES))]
        m = functools.reduce(jnp.maximum, lses)
        ws = [jnp.exp(l - m) for l in lses]
        num = sum(w * ob_ref[b, r, :] for b, w in enumerate(ws))
        o_ref[0, r, :] = (num / sum(ws)).astype(o_ref.dtype)
        return carry

    lax.fori_loop(0, seq // DIL_MERGE_ROWS, merge, 0)


def _dilated_attention(q, k, v, bias):
    b, s, w = q.shape
    npair = w // LANES
    nbr = len(DIL_BRANCHES)
    for _, dil in DIL_BRANCHES:
        nb = s // dil // DIL_BLOCK
        assert nb * dil * DIL_BLOCK == s and nb % DIL_UNROLL == 0 and nb & (nb - 1) == 0, (s, dil)
    assert s % DIL_MERGE_ROWS == 0
    spec = pl.BlockSpec((1, s, LANES), lambda i, p: (i, 0, p))
    return pl.pallas_call(
        functools.partial(_dil_kernel, seq=s),
        grid=(b, npair),
        in_specs=[spec, spec, spec,
                  pl.BlockSpec((PAIR, nbr, 2 * DIL_BLOCK, DIL_BLOCK), lambda i, p: (p, 0, 0, 0))],
        out_specs=spec,
        out_shape=jax.ShapeDtypeStruct((b, s, w), BF16),
        scratch_shapes=[pltpu.VMEM((3, s, LANES), F32),
                        pltpu.VMEM((nbr, s, LANES), F32),
                        pltpu.VMEM((nbr, s, LANES), F32)],
        compiler_params=_cparams(("arbitrary", "arbitrary")),
        name="dilated_attention",
    )(q, k, v, bias)


def _outproj_ln_kernel(*refs, n_in, alpha):
    a_refs, w_refs = refs[:n_in], refs[n_in:2 * n_in]
    x_ref, g_ref, b_ref, o_ref = refs[2 * n_in:]
    acc = _dot(a_refs[0][...], w_refs[0][...])
    for a_ref, w_ref in zip(a_refs[1:], w_refs[1:]):
        acc = acc + _dot(a_ref[...], w_ref[...])
    o_ref[...] = _layer_norm(alpha * x_ref[...] + acc, g_ref[...], b_ref[...])


def _outproj_ln(acts, weights, x2, g, b, alpha, tm):
    t, d = x2.shape
    n_in = len(acts)
    in_specs = ([pl.BlockSpec((tm, a.shape[1]), lambda i: (i, 0)) for a in acts]
                + [pl.BlockSpec(w.shape, lambda i: (0, 0)) for w in weights]
                + [pl.BlockSpec((tm, d), lambda i: (i, 0)),
                   pl.BlockSpec((1, d), lambda i: (0, 0)), pl.BlockSpec((1, d), lambda i: (0, 0))])
    return pl.pallas_call(
        functools.partial(_outproj_ln_kernel, n_in=n_in, alpha=alpha),
        grid=(t // tm,),
        in_specs=in_specs,
        out_specs=pl.BlockSpec((tm, d), lambda i: (i, 0)),
        out_shape=jax.ShapeDtypeStruct((t, d), F32),
        compiler_params=_cparams(("parallel",)),
        name="out_proj_ln",
    )(*acts, *weights, x2, g.reshape(1, d), b.reshape(1, d))


def _ffn_kernel(x_ref, wg_ref, wu_ref, wd_ref, g_ref, b_ref, o_ref, acc_ref, *, alpha):
    f = pl.program_id(1)

    @pl.when(f == 0)
    def _():
        acc_ref[...] = jnp.zeros(acc_ref.shape, F32)

    xb = x_ref[...].astype(BF16)
    gate = _dot(xb, wg_ref[...])
    up = _dot(xb, wu_ref[...])
    acc_ref[...] += _dot((_silu(gate) * up).astype(BF16), wd_ref[...])

    @pl.when(f == pl.num_programs(1) - 1)
    def _():
        o_ref[...] = _layer_norm(alpha * x_ref[...] + acc_ref[...], g_ref[...], b_ref[...])


def _ffn_ln(x2, w_in, w_down, g, b, alpha, tm, tf):
    t, d = x2.shape
    ff = w_down.shape[0]
    nf = ff // tf
    return pl.pallas_call(
        functools.partial(_ffn_kernel, alpha=alpha),
        grid=(t // tm, nf),
        in_specs=[pl.BlockSpec((tm, d), lambda i, f: (i, 0)),
                  pl.BlockSpec((d, tf), lambda i, f: (0, f)),
                  pl.BlockSpec((d, tf), lambda i, f: (0, f + nf)),
                  pl.BlockSpec((tf, d), lambda i, f: (f, 0)),
                  pl.BlockSpec((1, d), lambda i, f: (0, 0)),
                  pl.BlockSpec((1, d), lambda i, f: (0, 0))],
        out_specs=pl.BlockSpec((tm, d), lambda i, f: (i, 0)),
        out_shape=jax.ShapeDtypeStruct((t, d), F32),
        scratch_shapes=[pltpu.VMEM((tm, d), F32)],
        compiler_params=_cparams(("parallel", "arbitrary")),
        name="ffn_ln",
    )(x2, w_in, w_in, w_down, g.reshape(1, d), b.reshape(1, d))


MOE_TM = 1024
MOE_SUB = 64


def _router_kernel(x_ref, wr_ref, c_ref, pos_ref, post_ref, cnt_ref):
    tm = x_ref.shape[0]
    logits = _dot(x_ref[...], wr_ref[...], precision=HIGHEST)
    lanef = lax.broadcasted_iota(jnp.int32, logits.shape, 1).astype(F32)
    cur = jnp.where(lanef < N_EXPERTS, logits, -jnp.inf)
    vals, hits = [], []
    for _ in range(MOE_TOPK):
        best = jnp.max(cur, axis=1, keepdims=True)
        first = jnp.min(jnp.where(cur == best, lanef, float(LANES)), axis=1, keepdims=True)
        hit = lanef == first
        vals.append(best)
        hits.append(hit)
        cur = jnp.where(hit, -jnp.inf, cur)
    es = [jnp.exp(v - vals[0]) for v in vals]
    den = sum(es)
    comb = jnp.zeros(logits.shape, F32)
    routed = jnp.zeros(logits.shape, F32)
    for e, hit in zip(es, hits):
        comb = jnp.where(hit, e / den, comb)
        routed = jnp.where(hit, 1.0, routed)
    c_ref[...] = comb
    r = lax.broadcasted_iota(jnp.int32, (tm, tm), 0)
    c = lax.broadcasted_iota(jnp.int32, (tm, tm), 1)
    incl = _dot(jnp.where(c <= r, 1.0, 0.0).astype(BF16), routed.astype(BF16))
    pos = jnp.where(routed > 0.5, incl - 1.0, -1.0)
    pos_ref[...] = pos
    post_ref[0] = pos.T[:SUBLANES, :]
    cnt_ref[...] = jnp.broadcast_to(incl[tm - 1:tm, :], cnt_ref.shape)


def _router(x2, w_router, tm):
    t, d = x2.shape
    nt = t // tm
    wr = jnp.pad(w_router, ((0, 0), (0, LANES - N_EXPERTS)))
    return pl.pallas_call(
        _router_kernel,
        grid=(nt,),
        in_specs=[pl.BlockSpec((tm, d), lambda i: (i, 0)),
                  pl.BlockSpec((d, LANES), lambda i: (0, 0))],
        out_specs=[pl.BlockSpec((tm, LANES), lambda i: (i, 0)),
                   pl.BlockSpec((tm, LANES), lambda i: (i, 0)),
                   pl.BlockSpec((1, SUBLANES, tm), lambda i: (i, 0, 0)),
                   pl.BlockSpec((SUBLANES, LANES), lambda i: (i, 0))],
        out_shape=[jax.ShapeDtypeStruct((t, LANES), F32),
                   jax.ShapeDtypeStruct((t, LANES), F32),
                   jax.ShapeDtypeStruct((nt, SUBLANES, tm), F32),
                   jax.ShapeDtypeStruct((nt * SUBLANES, LANES), F32)],
        compiler_params=_cparams(("parallel",)),
        name="moe_router",
    )(x2, wr)


def _moe_kernel(nsub_ref, x_ref, c_ref, pos_ref, post_ref, wg_ref, wu_ref, wd_ref, g_ref, b_ref, o_ref,
                xb_ref, xs_ref, gs_ref, oe_ref, *, alpha):
    i = pl.program_id(0)
    e = pl.program_id(1)
    f = pl.program_id(2)
    last_f = pl.num_programs(2) - 1
    tm = x_ref.shape[0]
    sub = MOE_SUB
    n = nsub_ref[i * N_EXPERTS + e]
    lane = lax.broadcasted_iota(jnp.int32, (1, LANES), 1)

    @pl.when((e == 0) & (f == 0))
    def _():
        o_ref[...] = jnp.zeros(o_ref.shape, F32)
        xb_ref[...] = x_ref[...].astype(BF16)

    def for_subtiles(fn):
        def quad(s, carry):
            fn(pl.multiple_of(s * (4 * sub), 4 * sub), 4 * sub)
            return carry

        nq = n >> 2
        lax.fori_loop(0, nq, quad, 0)

        @pl.when((n & 2) != 0)
        def _():
            fn(pl.multiple_of(nq * (4 * sub), 2 * sub), 2 * sub)

        @pl.when((n & 1) != 0)
        def _():
            fn(pl.multiple_of((n - 1) * sub, sub), sub)

    @pl.when(f == 0)
    def _():
        comb = c_ref[...]
        c_hi = comb.astype(BF16)
        c_lo = (comb - c_hi.astype(F32)).astype(BF16)
        prow = post_ref[0, pl.ds(e, 1), :]

        def gather(r0, rows):
            want = (lax.broadcasted_iota(jnp.int32, (rows, tm), 0) + r0).astype(F32)
            sel = jnp.where(prow == want, 1.0, 0.0).astype(BF16)
            xs_ref[pl.ds(r0, rows), :] = _dot(sel, xb_ref[...]).astype(BF16)
            cg = _dot(sel, c_hi) + _dot(sel, c_lo)
            ce = jnp.sum(jnp.where(lane == e, cg, 0.0), axis=1, keepdims=True)
            gs_ref[pl.ds(r0, rows), :] = jnp.broadcast_to(ce, (rows, LANES))

        for_subtiles(gather)

    def ffn(r0, rows):
        xs = xs_ref[pl.ds(r0, rows), :]
        gate = _dot(xs, wg_ref[0])
        up = _dot(xs, wu_ref[0])
        act = (_silu(gate) * up * gs_ref[pl.ds(r0, rows), 0:1]).astype(BF16)
        contrib = _dot(act, wd_ref[0])

        @pl.when(f == 0)
        def _():
            oe_ref[pl.ds(r0, rows), :] = contrib

        @pl.when(f > 0)
        def _():
            oe_ref[pl.ds(r0, rows), :] += contrib

    for_subtiles(ffn)

    @pl.when(f == last_f)
    def _():
        pos = pos_ref[...]
        pcol = jnp.sum(jnp.where(lane == e, pos, 0.0), axis=1, keepdims=True)

        def scatter(r0, rows):
            want = (lax.broadcasted_iota(jnp.int32, (tm, rows), 1) + r0).astype(F32)
            sel_t = jnp.where(pcol == want, 1.0, 0.0).astype(BF16)
            o_ref[...] += _dot(sel_t, oe_ref[pl.ds(r0, rows), :].astype(BF16))

        for_subtiles(scatter)

    @pl.when((e == pl.num_programs(1) - 1) & (f == last_f))
    def _():
        o_ref[...] = _layer_norm(alpha * x_ref[...] + o_ref[...], g_ref[...], b_ref[...])


def _moe_ln(x2, routing, w_in, w_down, g, b, alpha, tm, tf):
    comb, pos, pos_t, cnt = routing
    t, d = x2.shape
    ne, ff, _ = w_down.shape
    nf = ff // tf
    nt = t // tm
    counts = cnt.reshape(nt, SUBLANES, LANES)[:, 0, :ne]
    nsub = jnp.ceil(counts / MOE_SUB).astype(jnp.int32).reshape(nt * ne)
    grid_spec = pltpu.PrefetchScalarGridSpec(
        num_scalar_prefetch=1,
        grid=(nt, ne, nf),
        in_specs=[pl.BlockSpec((tm, d), lambda i, e, f, ns: (i, 0)),
                  pl.BlockSpec((tm, LANES), lambda i, e, f, ns: (i, 0)),
                  pl.BlockSpec((tm, LANES), lambda i, e, f, ns: (i, 0)),
                  pl.BlockSpec((1, SUBLANES, tm), lambda i, e, f, ns: (i, 0, 0)),
                  pl.BlockSpec((1, d, tf), lambda i, e, f, ns: (e, 0, f)),
                  pl.BlockSpec((1, d, tf), lambda i, e, f, ns: (e, 0, f + nf)),
                  pl.BlockSpec((1, tf, d), lambda i, e, f, ns: (e, f, 0)),
                  pl.BlockSpec((1, d), lambda i, e, f, ns: (0, 0)),
                  pl.BlockSpec((1, d), lambda i, e, f, ns: (0, 0))],
        out_specs=pl.BlockSpec((tm, d), lambda i, e, f, ns: (i, 0)),
        scratch_shapes=[pltpu.VMEM((tm, d), BF16),
                        pltpu.VMEM((tm, d), BF16),
                        pltpu.VMEM((tm, LANES), F32),
                        pltpu.VMEM((tm, d), F32)])
    return pl.pallas_call(
        functools.partial(_moe_kernel, alpha=alpha),
        grid_spec=grid_spec,
        out_shape=jax.ShapeDtypeStruct((t, d), F32),
        compiler_params=pltpu.CompilerParams(dimension_semantics=("parallel", "arbitrary", "arbitrary"),
                                             vmem_limit_bytes=MOE_VMEM_LIMIT_BYTES),
        name="moe_ln",
    )(nsub, x2, comb, pos, pos_t, w_in, w_in, w_down, g.reshape(1, d), b.reshape(1, d))


def _tile(t, want):
    while t % want:
        want //= 2
    return want


def _even_layer(x2, bsz, seq, w_in, conv_w, conv_b, dt_bias, a_log, d_skip, norm_g, w_out,
                moba_bias, ln_g, ln_b, alpha):
    t = x2.shape[0]
    w_cat = jnp.concatenate(
        [w_in[:, :EVEN_MAIN], jnp.pad(w_in[:, EVEN_MAIN:], ((0, 0), (0, LANES - SSM_HEADS)))], axis=1
    ).astype(BF16)
    bounds = (0, MOBA_W, 2 * MOBA_W, 3 * MOBA_W, 3 * MOBA_W + SSM_D_INNER, EVEN_MAIN, EVEN_MAIN + LANES)
    segs = tuple(zip(bounds[:-1], bounds[1:]))
    q, k, v, z, xbc, dt_raw = _project(x2, w_cat, segs, (BF16, BF16, BF16, F32, F32, F32), _tile(t, 256))
    r3 = lambda a: a.reshape(bsz, seq, a.shape[-1])
    attn = _moba_attention(r3(q), r3(k), r3(v), moba_bias).reshape(t, MOBA_W)
    ssm = _ssd_mixer(r3(xbc), r3(z), r3(dt_raw), conv_w, conv_b, dt_bias, a_log, d_skip,
                     norm_g).reshape(t, SSM_D_INNER)
    w_o = w_out.astype(BF16)
    return _outproj_ln([attn, ssm], [w_o[:MOBA_W], w_o[MOBA_W:]], x2, ln_g, ln_b, alpha, _tile(t, 512))


def _odd_layer(x2, bsz, seq, w_qkv, w_out, dil_bias, ln_g, ln_b, alpha):
    t = x2.shape[0]
    w = DIL_HEADS * HEAD_DIM
    segs = ((0, w), (w, 2 * w), (2 * w, 3 * w))
    q, k, v = _project(x2, w_qkv.astype(BF16), segs, (BF16, BF16, BF16), _tile(t, 256))
    r3 = lambda a: a.reshape(bsz, seq, w)
    merged = _dilated_attention(r3(q), r3(k), r3(v), dil_bias).reshape(t, w)
    return _outproj_ln([merged], [w_out.astype(BF16)], x2, ln_g, ln_b, alpha, _tile(t, 512))


def kernel(x, rel_bias, ln_mix_g, ln_mix_b, ln_ffn_g, ln_ffn_b, even_w_in, even_conv_w, even_conv_b,
           even_dt_bias, even_A_log, even_D, even_norm_g, even_w_out, ffn_w_in, ffn_w_down,
           odd_w_qkv, odd_w_out, moe_router, moe_w_in, moe_w_down):
    bsz, seq, d = x.shape
    depth = ln_mix_g.shape[0]
    alpha = (2.0 * depth) ** 0.25
    t = bsz * seq
    x2 = x.reshape(t, d)
    moba_bias = _bias_tiles(rel_bias, _moba_bucket_idx(seq // MOBA_BLOCK), MOBA_HEADS)
    dil_bias = _bias_tiles(rel_bias, _dil_bucket_idx(), DIL_HEADS)
    for l in range(depth):
        e = l // 2
        if l % 2 == 0:
            x2 = _even_layer(x2, bsz, seq, even_w_in[e], even_conv_w[e], even_conv_b[e], even_dt_bias[e],
                             even_A_log[e], even_D[e], even_norm_g[e], even_w_out[e], moba_bias,
                             ln_mix_g[l], ln_mix_b[l], alpha)
            x2 = _ffn_ln(x2, ffn_w_in[e].astype(BF16), ffn_w_down[e].astype(BF16), ln_ffn_g[l], ln_ffn_b[l],
                         alpha, _tile(t, 512), FFN_DENSE // 2)
        else:
            x2 = _odd_layer(x2, bsz, seq, odd_w_qkv[e], odd_w_out[e], dil_bias, ln_mix_g[l], ln_mix_b[l], alpha)
            tm = _tile(t, MOE_TM)
            routing = _router(x2, moe_router[e], tm)
            x2 = _moe_ln(x2, routing, moe_w_in[e].astype(BF16), moe_w_down[e].astype(BF16), ln_ffn_g[l],
                         ln_ffn_b[l], alpha, tm, FFN_EXPERT // 4)
    return x2.reshape(bsz, seq, d)
```
